```python
import math
import jax, jax.numpy as jnp
from jax import lax
import numpy as np

D_MODEL = 1024
BATCH = 8
SEQ = 4096
DEPTH = 1

S5_GROUP = 16
S5_WIDTH = D_MODEL // 2
S5_GROUPS = S5_WIDTH // S5_GROUP
S5_STATE = 64
S5_DT_MIN = 0.001
S5_DT_MAX = 0.1
GLA_HEADS = 4
GLA_VAL_WIDTH = D_MODEL // 2
GLA_DV = GLA_VAL_WIDTH // GLA_HEADS
GLA_DK = GLA_DV // 2
GLA_KEY_WIDTH = GLA_HEADS * GLA_DK
GLA_GATE_RANK = 16
GLA_TAU = 16.0
GLA_CHUNK = 64
D_FF = ((8 * D_MODEL // 3 + 255) // 256) * 256
EPS = 1e-6
IN_SIZES = (S5_WIDTH, GLA_KEY_WIDTH, GLA_KEY_WIDTH, GLA_VAL_WIDTH, GLA_VAL_WIDTH,
            GLA_GATE_RANK, D_MODEL, D_MODEL)
IN_COLS = sum(IN_SIZES)

kernel_name = "hybrid_s5_gla_macaron_block"


def rms_norm(x, g):
    xf = x.astype(jnp.float32)
    y = xf * lax.rsqrt(jnp.mean(xf * xf, axis=-1, keepdims=True) + EPS)
    return (y * g.astype(jnp.float32)).astype(x.dtype)


def swiglu(x, w1, w3, w2):
    return (jax.nn.silu(x @ w1) * (x @ w3)) @ w2


def _ssm_combine(e1, e2):
    ar1, ai1, br1, bi1 = e1
    ar2, ai2, br2, bi2 = e2
    return (ar1 * ar2 - ai1 * ai2,
            ar1 * ai2 + ai1 * ar2,
            ar2 * br1 - ai2 * bi1 + br2,
            ar2 * bi1 + ai2 * br1 + bi2)


def s5_mixer(u, lam_re, lam_im, log_dt, b_re, b_im, c_re, c_im, d_skip, w_glu, b_glu):
    bsz, seq, _ = u.shape
    f32 = jnp.float32
    lam_re = lam_re.astype(f32)
    lam_im = lam_im.astype(f32)
    dt = jnp.exp(log_dt.astype(f32))[:, None]
    mag = jnp.exp(lam_re * dt)
    ar = mag * jnp.cos(lam_im * dt)
    ai = mag * jnp.sin(lam_im * dt)
    den = lam_re * lam_re + lam_im * lam_im
    nr = ar - 1.0
    fr = (nr * lam_re + ai * lam_im) / den
    fi = (ai * lam_re - nr * lam_im) / den
    b_re = b_re.astype(f32)
    b_im = b_im.astype(f32)
    bbar_re = fr[:, :, None] * b_re - fi[:, :, None] * b_im
    bbar_im = fr[:, :, None] * b_im + fi[:, :, None] * b_re
    ug = u.astype(f32).reshape(bsz, seq, S5_GROUPS, S5_GROUP)
    bu_re = jnp.einsum('blgh,gph->lbgp', ug, bbar_re)
    bu_im = jnp.einsum('blgh,gph->lbgp', ug, bbar_im)
    a_re = jnp.broadcast_to(ar, (seq, 1, S5_GROUPS, S5_STATE))
    a_im = jnp.broadcast_to(ai, (seq, 1, S5_GROUPS, S5_STATE))
    _, _, xr, xi = lax.associative_scan(_ssm_combine, (a_re, a_im, bu_re, bu_im), axis=0)
    y = (jnp.einsum('ghp,lbgp->blgh', c_re.astype(f32), xr)
         - jnp.einsum('ghp,lbgp->blgh', c_im.astype(f32), xi)
         + d_skip.astype(f32) * ug)
    y = y.reshape(bsz, seq, S5_WIDTH).astype(u.dtype)
    z = jax.nn.gelu(y)
    return z * jax.nn.sigmoid(z @ w_glu + b_glu)


def gla_mixer(q, k, v, r, a_low, w_a_up, b_a_up, g_norm):
    bsz, seq, _ = q.shape
    n_chunks = seq // GLA_CHUNK
    f32 = jnp.float32
    shp_k = (bsz, n_chunks, GLA_CHUNK, GLA_HEADS, GLA_DK)
    shp_v = (bsz, n_chunks, GLA_CHUNK, GLA_HEADS, GLA_DV)
    qc = q.astype(f32).reshape(shp_k) * (GLA_DK ** -0.5)
    kc = k.astype(f32).reshape(shp_k)
    vc = v.astype(f32).reshape(shp_v)
    log_a = jax.nn.log_sigmoid((a_low @ w_a_up + b_a_up).astype(f32)) / GLA_TAU
    bcum = jnp.cumsum(log_a.reshape(shp_k), axis=2)
    b_last = bcum[:, :, -1]
    q_t = qc * jnp.exp(bcum)
    k_t = kc * jnp.exp(-bcum)
    scores = jnp.einsum('bnthd,bnshd->bnhts', q_t, k_t)
    causal = jnp.tril(jnp.ones((GLA_CHUNK, GLA_CHUNK), dtype=bool))
    scores = jnp.where(causal, scores, 0.0)
    o_intra = jnp.einsum('bnhts,bnshv->bnthv', scores, vc)
    k_end = kc * jnp.exp(b_last[:, :, None] - bcum)
    d_state = jnp.einsum('bnshd,bnshv->bnhdv', k_end, vc)
    decay = jnp.exp(b_last)

    def step(state, inp):
        dec, ds = inp
        return dec[..., None] * state + ds, state

    s0 = jnp.zeros((bsz, GLA_HEADS, GLA_DK, GLA_DV), f32)
    _, s_prev = lax.scan(step, s0, (jnp.moveaxis(decay, 1, 0), jnp.moveaxis(d_state, 1, 0)))
    s_prev = jnp.moveaxis(s_prev, 0, 1)
    o_inter = jnp.einsum('bnthd,bnhdv->bnthv', q_t, s_prev)
    o = (o_intra + o_inter).reshape(bsz, seq, GLA_HEADS, GLA_DV)
    o = o * lax.rsqrt(jnp.mean(o * o, axis=-1, keepdims=True) + EPS)
    o = o.reshape(bsz, seq, GLA_VAL_WIDTH) * g_norm.astype(f32)
    return (o * jax.nn.silu(r.astype(f32))).astype(q.dtype)


def setup_inputs(seed: int = 0) -> dict:
    key = jax.random.key(seed)
    ks = jax.random.split(key, 32)
    f32 = jnp.float32
    L, G, P, H = DEPTH, S5_GROUPS, S5_STATE, S5_GROUP

    def nrm(k, shape, scale):
        return jax.random.normal(k, shape, f32) * scale

    def gain(k, shape):
        return 1.0 + 0.01 * jax.random.normal(k, shape, f32)

    n_idx = jnp.arange(P, dtype=f32)
    return {
        "x": nrm(ks[0], (BATCH, SEQ, D_MODEL), 1.0),
        "ffn1_norm": gain(ks[1], (L, D_MODEL)),
        "ffn1_w1": nrm(ks[2], (L, D_MODEL, D_FF), D_MODEL ** -0.5),
        "ffn1_w3": nrm(ks[3], (L, D_MODEL, D_FF), D_MODEL ** -0.5),
        "ffn1_w2": nrm(ks[4], (L, D_FF, D_MODEL), D_FF ** -0.5),
        "mix_norm": gain(ks[5], (L, D_MODEL)),
        "w_in": nrm(ks[6], (L, D_MODEL, IN_COLS), D_MODEL ** -0.5),
        "s5_lambda_re": -0.5 + 0.01 * jax.random.normal(ks[7], (L, G, P), f32),
        "s5_lambda_im": math.pi * n_idx + 0.01 * jax.random.normal(ks[8], (L, G, P), f32),
        "s5_log_dt": jax.random.uniform(ks[9], (L, G), f32, math.log(S5_DT_MIN), math.log(S5_DT_MAX)),
        "s5_b_re": nrm(ks[10], (L, G, P, H), (2.0 * H) ** -0.5),
        "s5_b_im": nrm(ks[11], (L, G, P, H), (2.0 * H) ** -0.5),
        "s5_c_re": nrm(ks[12], (L, G, H, P), (2.0 * P) ** -0.5),
        "s5_c_im": nrm(ks[13], (L, G, H, P), (2.0 * P) ** -0.5),
        "s5_d": nrm(ks[14], (L, G, H), 1.0),
        "s5_glu_w": nrm(ks[15], (L, S5_WIDTH, S5_WIDTH), S5_WIDTH ** -0.5),
        "s5_glu_b": nrm(ks[16], (L, S5_WIDTH), 0.01),
        "gla_a_up_w": nrm(ks[17], (L, GLA_GATE_RANK, GLA_KEY_WIDTH), GLA_GATE_RANK ** -0.5),
        "gla_a_up_b": nrm(ks[18], (L, GLA_KEY_WIDTH), 0.1),
        "gla_out_norm": gain(ks[19], (L, GLA_VAL_WIDTH)),
        "proj_s5": nrm(ks[20], (L, S5_WIDTH, D_MODEL), S5_WIDTH ** -0.5),
        "proj_gla": nrm(ks[21], (L, GLA_VAL_WIDTH, D_MODEL), GLA_VAL_WIDTH ** -0.5),
        "w_out": nrm(ks[22], (L, D_MODEL, D_MODEL), D_MODEL ** -0.5),
        "ffn2_norm": gain(ks[23], (L, D_MODEL)),
        "ffn2_w1": nrm(ks[24], (L, D_MODEL, D_FF), D_MODEL ** -0.5),
        "ffn2_w3": nrm(ks[25], (L, D_MODEL, D_FF), D_MODEL ** -0.5),
        "ffn2_w2": nrm(ks[26], (L, D_FF, D_MODEL), D_FF ** -0.5),
        "final_norm": gain(ks[27], (D_MODEL,)),
    }


def reference(x, ffn1_norm, ffn1_w1, ffn1_w3, ffn1_w2, mix_norm, w_in,
              s5_lambda_re, s5_lambda_im, s5_log_dt, s5_b_re, s5_b_im, s5_c_re, s5_c_im,
              s5_d, s5_glu_w, s5_glu_b, gla_a_up_w, gla_a_up_b, gla_out_norm,
              proj_s5, proj_gla, w_out, ffn2_norm, ffn2_w1, ffn2_w3, ffn2_w2, final_norm):
    split_idx = list(np.cumsum(IN_SIZES)[:-1])
    h = x
    for l in range(DEPTH):
        h = h + 0.5 * swiglu(rms_norm(h, ffn1_norm[l]), ffn1_w1[l], ffn1_w3[l], ffn1_w2[l])
        u = rms_norm(h, mix_norm[l])
        s5_in, q, k, v, r, a_low, g_s5, g_gla = jnp.split(u @ w_in[l], split_idx, axis=-1)
        y_s5 = s5_mixer(s5_in, s5_lambda_re[l], s5_lambda_im[l], s5_log_dt[l],
                        s5_b_re[l], s5_b_im[l], s5_c_re[l], s5_c_im[l], s5_d[l],
                        s5_glu_w[l], s5_glu_b[l])
        y_gla = gla_mixer(q, k, v, r, a_low, gla_a_up_w[l], gla_a_up_b[l], gla_out_norm[l])
        merged = (jax.nn.sigmoid(g_s5) * (y_s5 @ proj_s5[l])
                  + jax.nn.sigmoid(g_gla) * (y_gla @ proj_gla[l]))
        h = h + merged @ w_out[l]
        h = h + 0.5 * swiglu(rms_norm(h, ffn2_norm[l]), ffn2_w1[l], ffn2_w3[l], ffn2_w2[l])
    return rms_norm(h, final_norm).astype(x.dtype)
```

```python
import functools
import math

import jax
import jax.numpy as jnp
from jax import lax
from jax.experimental import pallas as pl
from jax.experimental.pallas import tpu as pltpu

D_MODEL = 1024
S5_GROUP = 16
S5_WIDTH = 512
S5_GROUPS = 32
S5_STATE = 64
GLA_HEADS = 4
GLA_DV = 128
GLA_DK = 64
GLA_KEY_WIDTH = 256
GLA_VAL_WIDTH = 512
GLA_GATE_RANK = 16
GLA_TAU = 16.0
GLA_CHUNK = 64
D_FF = 2816
EPS = 1e-6
IN_SIZES = (S5_WIDTH, GLA_KEY_WIDTH, GLA_KEY_WIDTH, GLA_VAL_WIDTH, GLA_VAL_WIDTH,
            GLA_GATE_RANK, D_MODEL, D_MODEL)

LANES = 128
SUBLANES = 8
S5_BLOCKS = 4
S5_BLOCK_GROUPS = S5_GROUPS // S5_BLOCKS
S5_BLOCK_STATE = S5_BLOCK_GROUPS * S5_STATE
A_LOW_PAD = LANES
FFN_STEPS = 64
VMEM_LIMIT = 56 * 1024 * 1024

BF16 = jnp.bfloat16
F32 = jnp.float32

C_S5 = 0
C_Q = C_S5 + S5_WIDTH
C_K = C_Q + GLA_KEY_WIDTH
C_V = C_K + GLA_KEY_WIDTH
C_R = C_V + GLA_VAL_WIDTH
C_GS5 = C_R + GLA_VAL_WIDTH
C_GGLA = C_GS5 + D_MODEL
C_ALOW = C_GGLA + D_MODEL
IN_COLS_PAD = C_ALOW + A_LOW_PAD


def _dot(a, b):
    return jnp.dot(a, b, preferred_element_type=F32)


def _dot_nt(a, b):
    return lax.dot_general(a, b, (((1,), (1,)), ((), ())), preferred_element_type=F32)


def _dot_tn(a, b):
    return lax.dot_general(a, b, (((0,), (0,)), ((), ())), preferred_element_type=F32)


def _rms_norm(x, g):
    return x * lax.rsqrt(jnp.mean(x * x, axis=-1, keepdims=True) + EPS) * g


def _const_spec(shape):
    nd = len(shape)
    return pl.BlockSpec(shape, lambda *_: (0,) * nd, pipeline_mode=pl.Buffered(1))


def _ffn_body(x_ref, g_ref, w1_ref, w3_ref, w2_ref, gf_ref, o_ref, stage_ref, *,
              batch_major_in, final_norm, bsz, steps):
    rows = bsz * steps
    x = x_ref[...].reshape(rows, D_MODEL)
    xn = _rms_norm(x, g_ref[...]).astype(BF16)
    a = _dot(xn, w1_ref[...])
    b = _dot(xn, w3_ref[...])
    hmid = (a * jax.nn.sigmoid(a) * b).astype(BF16)
    y = x + 0.5 * _dot(hmid, w2_ref[...])
    if final_norm:
        y = _rms_norm(y, gf_ref[...])
    for j in range(D_MODEL // LANES):
        stage_ref[j] = y[:, j * LANES:(j + 1) * LANES]
    if batch_major_in:
        def put(t, carry):
            o_ref[t] = jnp.concatenate(
                [stage_ref[j, pl.ds(t, bsz, stride=steps), :] for j in range(D_MODEL // LANES)],
                axis=-1)
            return carry
        lax.fori_loop(0, steps, put, 0, unroll=8)
    else:
        for bi in range(bsz):
            o_ref[bi] = jnp.concatenate(
                [stage_ref[j, pl.ds(bi, steps, stride=bsz), :] for j in range(D_MODEL // LANES)],
                axis=-1)


def _ffn_call(x, g, w1, w3, w2, gf, *, batch_major_in, final_norm, bsz, seq):
    steps = min(FFN_STEPS, seq)
    bm_spec = pl.BlockSpec((bsz, steps, D_MODEL), lambda i: (0, i, 0))
    tm_spec = pl.BlockSpec((steps, bsz, D_MODEL), lambda i: (i, 0, 0))
    in_spec, out_spec = (bm_spec, tm_spec) if batch_major_in else (tm_spec, bm_spec)
    out_shape = (seq, bsz, D_MODEL) if batch_major_in else (bsz, seq, D_MODEL)
    return pl.pallas_call(
        functools.partial(_ffn_body, batch_major_in=batch_major_in, final_norm=final_norm,
                          bsz=bsz, steps=steps),
        grid=(seq // steps,),
        in_specs=[in_spec, _const_spec((1, D_MODEL)), _const_spec((D_MODEL, D_FF)),
                  _const_spec((D_MODEL, D_FF)), _const_spec((D_FF, D_MODEL)),
                  _const_spec((1, D_MODEL))],
        out_specs=out_spec,
        out_shape=jax.ShapeDtypeStruct(out_shape, F32),
        scratch_shapes=[pltpu.VMEM((D_MODEL // LANES, bsz * steps, LANES), F32)],
        compiler_params=pltpu.CompilerParams(
            dimension_semantics=("arbitrary",), vmem_limit_bytes=VMEM_LIMIT),
        name="ffn_final" if final_norm else "ffn",
    )(x, g, w1, w3, w2, gf)


def _mixer_body(h_ref, g_ref, w_in_ref, a_re_ref, a_im_ref, wb_ref, wc_ref, d_ref,
                glu_w_ref, glu_b_ref, a_up_w_ref, a_up_b_ref, gnorm_ref,
                p_s5_ref, p_gla_ref, w_out_ref, o_ref,
                s5_state, gla_state, bu_buf, loga_buf, bcum_buf, *, bsz, chunk):
    rows = bsz * chunk

    @pl.when(pl.program_id(0) == 0)
    def _():
        s5_state[...] = jnp.zeros_like(s5_state)
        gla_state[...] = jnp.zeros_like(gla_state)

    h = h_ref[...].reshape(rows, D_MODEL)
    u = _rms_norm(h, g_ref[...]).astype(BF16)

    def proj(col, width):
        return _dot(u, w_in_ref[:, col:col + width])

    s5_in = proj(C_S5, S5_WIDTH)
    s5_in_bf = s5_in.astype(BF16)
    y_parts = []
    for m in range(S5_BLOCKS):
        bu_buf[...] = _dot(s5_in_bf[:, m * LANES:(m + 1) * LANES], wb_ref[m])
        a_re = jnp.broadcast_to(a_re_ref[m], (bsz, S5_BLOCK_STATE))
        a_im = jnp.broadcast_to(a_im_ref[m], (bsz, S5_BLOCK_STATE))

        def step(t, carry, a_re=a_re, a_im=a_im):
            s_re, s_im = carry
            r0 = pl.multiple_of(t * bsz, bsz)
            n_re = a_re * s_re - a_im * s_im + bu_buf[pl.ds(r0, bsz), :S5_BLOCK_STATE]
            n_im = a_re * s_im + a_im * s_re + bu_buf[pl.ds(r0, bsz), S5_BLOCK_STATE:]
            bu_buf[pl.ds(r0, bsz), :S5_BLOCK_STATE] = n_re
            bu_buf[pl.ds(r0, bsz), S5_BLOCK_STATE:] = n_im
            return n_re, n_im

        s_re, s_im = lax.fori_loop(
            0, chunk, step,
            (s5_state[m, :, :S5_BLOCK_STATE], s5_state[m, :, S5_BLOCK_STATE:]), unroll=4)
        s5_state[m, :, :S5_BLOCK_STATE] = s_re
        s5_state[m, :, S5_BLOCK_STATE:] = s_im
        y_parts.append(_dot(bu_buf[...].astype(BF16), wc_ref[m]))
    y = jnp.concatenate(y_parts, axis=-1) + d_ref[...] * s5_in
    z = jax.nn.gelu(y)
    y_s5 = z * jax.nn.sigmoid(_dot(z.astype(BF16), glu_w_ref[...]) + glu_b_ref[...])

    a_low = proj(C_ALOW, A_LOW_PAD).astype(BF16)
    loga_buf[...] = jax.nn.log_sigmoid(_dot(a_low, a_up_w_ref[...]) + a_up_b_ref[...]) / GLA_TAU

    def cum_step(t, acc):
        r0 = pl.multiple_of(t * bsz, bsz)
        acc = acc + loga_buf[pl.ds(r0, bsz), :]
        bcum_buf[pl.ds(r0, bsz), :] = acc
        return acc

    b_last = lax.fori_loop(0, chunk, cum_step, jnp.zeros((bsz, GLA_KEY_WIDTH), F32), unroll=8)
    bcum = bcum_buf[...]
    b_last_rows = jnp.broadcast_to(b_last[None], (chunk, bsz, GLA_KEY_WIDTH)).reshape(
        rows, GLA_KEY_WIDTH)
    q = proj(C_Q, GLA_KEY_WIDTH) * (GLA_DK ** -0.5)
    k = proj(C_K, GLA_KEY_WIDTH)
    q_t = q * jnp.exp(bcum)
    k_t = (k * jnp.exp(-bcum)).astype(BF16)
    k_end = k * jnp.exp(b_last_rows - bcum)
    v = proj(C_V, GLA_VAL_WIDTH).astype(BF16)

    row = lax.broadcasted_iota(jnp.int32, (rows, rows), 0)
    col = lax.broadcasted_iota(jnp.int32, (rows, rows), 1)
    causal = ((row % bsz) == (col % bsz)) & (col <= row)
    own_batch = (col // GLA_DK) == (row % bsz)
    own_batch8 = own_batch[:bsz]
    key_lane = lax.broadcasted_iota(jnp.int32, (rows, GLA_KEY_WIDTH), 1) // GLA_DK

    def expand(xh):
        rep = xh + pltpu.roll(xh, 2 * GLA_DK, 1)
        rep = rep + pltpu.roll(rep, GLA_DK, 1)
        return jnp.concatenate([rep, rep], axis=-1)

    o_parts = []
    for hd in range(GLA_HEADS):
        head = key_lane == hd
        q_h = jnp.where(head, q_t, 0.0)
        scores = _dot_nt(q_h.astype(BF16), k_t)
        p = jnp.where(causal, scores, 0.0).astype(BF16)
        v_h = v[:, hd * GLA_DV:(hd + 1) * GLA_DV]
        o_h = _dot(p, v_h)
        q_exp = jnp.where(own_batch, expand(q_h), 0.0).astype(BF16)
        s_prev = gla_state[hd]
        o_h = o_h + _dot_nt(q_exp, s_prev.astype(BF16))
        k_exp = jnp.where(own_batch, expand(jnp.where(head, k_end, 0.0)), 0.0).astype(BF16)
        d_state = _dot_tn(v_h, k_exp)
        bl = expand(jnp.where(head[:bsz], b_last, 0.0))
        decay = jnp.exp(jnp.sum(jnp.where(own_batch8, bl, 0.0), axis=0, keepdims=True))
        gla_state[hd] = decay * s_prev + d_state
        o_parts.append(o_h * lax.rsqrt(jnp.mean(o_h * o_h, axis=-1, keepdims=True) + EPS))
    o = jnp.concatenate(o_parts, axis=-1) * gnorm_ref[...]
    r = proj(C_R, GLA_VAL_WIDTH)
    y_gla = o * (r * jax.nn.sigmoid(r))

    merged = (jax.nn.sigmoid(proj(C_GS5, D_MODEL)) * _dot(y_s5.astype(BF16), p_s5_ref[...])
              + jax.nn.sigmoid(proj(C_GGLA, D_MODEL)) * _dot(y_gla.astype(BF16), p_gla_ref[...]))
    out = h + _dot(merged.astype(BF16), w_out_ref[...])
    o_ref[...] = out.reshape(chunk, bsz, D_MODEL)


def _mixer_call(h, g, w_in, a_re, a_im, wb, wc, d_skip, glu_w, glu_b, a_up_w, a_up_b, gnorm,
                p_s5, p_gla, w_out, *, bsz, seq):
    chunk = GLA_CHUNK
    rows = bsz * chunk
    args = (g, w_in, a_re, a_im, wb, wc, d_skip, glu_w, glu_b, a_up_w, a_up_b, gnorm,
            p_s5, p_gla, w_out)
    tile_spec = pl.BlockSpec((chunk, bsz, D_MODEL), lambda i: (i, 0, 0))
    return pl.pallas_call(
        functools.partial(_mixer_body, bsz=bsz, chunk=chunk),
        grid=(seq // chunk,),
        in_specs=[tile_spec] + [_const_spec(a.shape) for a in args],
        out_specs=tile_spec,
        out_shape=jax.ShapeDtypeStruct((seq, bsz, D_MODEL), F32),
        scratch_shapes=[
            pltpu.VMEM((S5_BLOCKS, bsz, 2 * S5_BLOCK_STATE), F32),
            pltpu.VMEM((GLA_HEADS, GLA_DV, bsz * GLA_DK), F32),
            pltpu.VMEM((rows, 2 * S5_BLOCK_STATE), F32),
            pltpu.VMEM((rows, GLA_KEY_WIDTH), F32),
            pltpu.VMEM((rows, GLA_KEY_WIDTH), F32),
        ],
        compiler_params=pltpu.CompilerParams(
            dimension_semantics=("arbitrary",), vmem_limit_bytes=VMEM_LIMIT),
        name="mixer",
    )(h, *args)


def _s5_discretize(lam_re, lam_im, log_dt, b_re, b_im):
    dt = jnp.exp(log_dt)[:, None]
    mag = jnp.exp(lam_re * dt)
    ar = mag * jnp.cos(lam_im * dt)
    ai = mag * jnp.sin(lam_im * dt)
    den = lam_re * lam_re + lam_im * lam_im
    nr = ar - 1.0
    fr = (nr * lam_re + ai * lam_im) / den
    fi = (ai * lam_re - nr * lam_im) / den
    bbar_re = fr[:, :, None] * b_re - fi[:, :, None] * b_im
    bbar_im = fr[:, :, None] * b_im + fi[:, :, None] * b_re
    return ar, ai, bbar_re, bbar_im


def _pack_s5(ar, ai, bbar_re, bbar_im, c_re, c_im):
    eye = jnp.eye(S5_BLOCK_GROUPS, dtype=F32)
    nb, ng, ns, nh = S5_BLOCKS, S5_BLOCK_GROUPS, S5_STATE, S5_GROUP
    bb = jnp.stack([bbar_re, bbar_im]).reshape(2, nb, ng, ns, nh)
    wb = jnp.einsum('rmgph,gk->mghrkp', bb, eye).reshape(nb, ng * nh, 2 * ng * ns)
    cc = jnp.stack([c_re, -c_im]).reshape(2, nb, ng, nh, ns)
    wc = jnp.einsum('rmghp,gk->mrgpkh', cc, eye).reshape(nb, 2 * ng * ns, ng * nh)
    a_re = ar.reshape(nb, 1, ng * ns)
    a_im = ai.reshape(nb, 1, ng * ns)
    return a_re, a_im, wb.astype(BF16), wc.astype(BF16)


def kernel(x, ffn1_norm, ffn1_w1, ffn1_w3, ffn1_w2, mix_norm, w_in, s5_lambda_re, s5_lambda_im, s5_log_dt, s5_b_re, s5_b_im, s5_c_re, s5_c_im, s5_d, s5_glu_w, s5_glu_b, gla_a_up_w, gla_a_up_b, gla_out_norm, proj_s5, proj_gla, w_out, ffn2_norm, ffn2_w1, ffn2_w3, ffn2_w2, final_norm):
    bsz, seq, _ = x.shape
    assert bsz == SUBLANES and seq % GLA_CHUNK == 0 and ffn1_norm.shape[0] == 1
    l = 0
    row = lambda a: a.reshape(1, -1).astype(F32)

    h = _ffn_call(x, row(ffn1_norm[l]), ffn1_w1[l].astype(BF16), ffn1_w3[l].astype(BF16),
                  ffn1_w2[l].astype(BF16), row(final_norm),
                  batch_major_in=True, final_norm=False, bsz=bsz, seq=seq)

    offs = [0]
    for s in IN_SIZES:
        offs.append(offs[-1] + s)
    seg = [w_in[l][:, offs[i]:offs[i + 1]] for i in range(len(IN_SIZES))]
    w_in_p = jnp.concatenate(
        seg[:5] + seg[6:] + [seg[5], jnp.zeros((D_MODEL, A_LOW_PAD - GLA_GATE_RANK), F32)],
        axis=1).astype(BF16)
    a_up_w = jnp.concatenate(
        [gla_a_up_w[l], jnp.zeros((A_LOW_PAD - GLA_GATE_RANK, GLA_KEY_WIDTH), F32)],
        axis=0).astype(BF16)
    ar, ai, bbar_re, bbar_im = _s5_discretize(
        s5_lambda_re[l], s5_lambda_im[l], s5_log_dt[l], s5_b_re[l], s5_b_im[l])
    a_re, a_im, wb, wc = _pack_s5(ar, ai, bbar_re, bbar_im, s5_c_re[l], s5_c_im[l])

    h = _mixer_call(h, row(mix_norm[l]), w_in_p, a_re, a_im, wb, wc, row(s5_d[l]),
                    s5_glu_w[l].astype(BF16), row(s5_glu_b[l]), a_up_w, row(gla_a_up_b[l]),
                    row(gla_out_norm[l]), proj_s5[l].astype(BF16), proj_gla[l].astype(BF16),
                    w_out[l].astype(BF16), bsz=bsz, seq=seq)

    return _ffn_call(h, row(ffn2_norm[l]), ffn2_w1[l].astype(BF16), ffn2_w3[l].astype(BF16),
                     ffn2_w2[l].astype(BF16), row(final_norm),
                     batch_major_in=False, final_norm=True, bsz=bsz, seq=seq)
```

```python
import functools
import math

import jax
import jax.numpy as jnp
from jax import lax
from jax.experimental import pallas as pl
from jax.experimental.pallas import tpu as pltpu

D_MODEL = 1024
S5_GROUP = 16
S5_WIDTH = 512
S5_GROUPS = 32
S5_STATE = 64
GLA_HEADS = 4
GLA_DV = 128
GLA_DK = 64
GLA_KEY_WIDTH = 256
GLA_VAL_WIDTH = 512
GLA_GATE_RANK = 16
GLA_TAU = 16.0
GLA_CHUNK = 64
D_FF = 2816
EPS = 1e-6
IN_SIZES = (S5_WIDTH, GLA_KEY_WIDTH, GLA_KEY_WIDTH, GLA_VAL_WIDTH, GLA_VAL_WIDTH,
            GLA_GATE_RANK, D_MODEL, D_MODEL)

LANES = 128
SUBLANES = 8
S5_BLOCKS = 4
S5_BLOCK_GROUPS = S5_GROUPS // S5_BLOCKS
S5_BLOCK_STATE = S5_BLOCK_GROUPS * S5_STATE
A_LOW_PAD = LANES
FFN_STEPS = 64
VMEM_LIMIT = 56 * 1024 * 1024

BF16 = jnp.bfloat16
F32 = jnp.float32

C_S5 = 0
C_Q = C_S5 + S5_WIDTH
C_K = C_Q + GLA_KEY_WIDTH
C_V = C_K + GLA_KEY_WIDTH
C_R = C_V + GLA_VAL_WIDTH
C_GS5 = C_R + GLA_VAL_WIDTH
C_GGLA = C_GS5 + D_MODEL
C_ALOW = C_GGLA + D_MODEL
IN_COLS_PAD = C_ALOW + A_LOW_PAD


def _dot(a, b):
    return jnp.dot(a, b, preferred_element_type=F32)


def _dot_nt(a, b):
    return lax.dot_general(a, b, (((1,), (1,)), ((), ())), preferred_element_type=F32)


def _dot_tn(a, b):
    return lax.dot_general(a, b, (((0,), (0,)), ((), ())), preferred_element_type=F32)


def _rms_norm(x, g):
    return x * lax.rsqrt(jnp.mean(x * x, axis=-1, keepdims=True) + EPS) * g


def _const_spec(shape):
    nd = len(shape)
    return pl.BlockSpec(shape, lambda *_: (0,) * nd, pipeline_mode=pl.Buffered(1))


def _ffn_body(x_ref, g_ref, w1_ref, w3_ref, w2_ref, gf_ref, o_ref, stage_ref, *,
              batch_major_in, final_norm, bsz, steps):
    rows = bsz * steps
    x = x_ref[...].reshape(rows, D_MODEL)
    xn = _rms_norm(x, g_ref[...]).astype(BF16)
    a = _dot(xn, w1_ref[...])
    b = _dot(xn, w3_ref[...])
    hmid = (a * jax.nn.sigmoid(a) * b).astype(BF16)
    y = x + 0.5 * _dot(hmid, w2_ref[...])
    if final_norm:
        y = _rms_norm(y, gf_ref[...])
    for j in range(D_MODEL // LANES):
        stage_ref[j] = y[:, j * LANES:(j + 1) * LANES]
    if batch_major_in:
        def put(t, carry):
            o_ref[t] = jnp.concatenate(
                [stage_ref[j, pl.ds(t, bsz, stride=steps), :] for j in range(D_MODEL // LANES)],
                axis=-1)
            return carry
        lax.fori_loop(0, steps, put, 0, unroll=8)
    else:
        for bi in range(bsz):
            o_ref[bi] = jnp.concatenate(
                [stage_ref[j, pl.ds(bi, steps, stride=bsz), :] for j in range(D_MODEL // LANES)],
                axis=-1)


def _ffn_call(x, g, w1, w3, w2, gf, *, batch_major_in, final_norm, bsz, seq):
    steps = min(FFN_STEPS, seq)
    bm_spec = pl.BlockSpec((bsz, steps, D_MODEL), lambda i: (0, i, 0))
    tm_spec = pl.BlockSpec((steps, bsz, D_MODEL), lambda i: (i, 0, 0))
    in_spec, out_spec = (bm_spec, tm_spec) if batch_major_in else (tm_spec, bm_spec)
    out_shape = (seq, bsz, D_MODEL) if batch_major_in else (bsz, seq, D_MODEL)
    return pl.pallas_call(
        functools.partial(_ffn_body, batch_major_in=batch_major_in, final_norm=final_norm,
                          bsz=bsz, steps=steps),
        grid=(seq // steps,),
        in_specs=[in_spec, _const_spec((1, D_MODEL)), _const_spec((D_MODEL, D_FF)),
                  _const_spec((D_MODEL, D_FF)), _const_spec((D_FF, D_MODEL)),
                  _const_spec((1, D_MODEL))],
        out_specs=out_spec,
        out_shape=jax.ShapeDtypeStruct(out_shape, F32),
        scratch_shapes=[pltpu.VMEM((D_MODEL // LANES, bsz * steps, LANES), F32)],
        compiler_params=pltpu.CompilerParams(
            dimension_semantics=("arbitrary",), vmem_limit_bytes=VMEM_LIMIT),
        name="ffn_final" if final_norm else "ffn",
    )(x, g, w1, w3, w2, gf)


def _mixer_body(h_ref, g_ref, w_in_ref, a_re_ref, a_im_ref, wb_ref, wc_ref, d_ref,
                glu_w_ref, glu_b_ref, a_up_w_ref, a_up_b_ref, gnorm_ref,
                p_s5_ref, p_gla_ref, w_out_ref, o_ref,
                s5_state, gla_state, bu_buf, loga_buf, bcum_buf, *, bsz, chunk):
    rows = bsz * chunk

    @pl.when(pl.program_id(0) == 0)
    def _():
        s5_state[...] = jnp.zeros_like(s5_state)
        gla_state[...] = jnp.zeros_like(gla_state)

    h = h_ref[...].reshape(rows, D_MODEL)
    u = _rms_norm(h, g_ref[...]).astype(BF16)

    def proj(col, width):
        return _dot(u, w_in_ref[:, col:col + width])

    s5_in = proj(C_S5, S5_WIDTH)
    s5_in_bf = s5_in.astype(BF16)
    y_parts = []
    for m in range(S5_BLOCKS):
        buf = bu_buf.at[m]
        buf[...] = _dot(s5_in_bf[:, m * LANES:(m + 1) * LANES], wb_ref[m])
        a_re = jnp.broadcast_to(a_re_ref[m], (bsz, S5_BLOCK_STATE))
        a_im = jnp.broadcast_to(a_im_ref[m], (bsz, S5_BLOCK_STATE))
        s_re = s5_state[m, :, :S5_BLOCK_STATE]
        s_im = s5_state[m, :, S5_BLOCK_STATE:]
        for t in range(chunk):
            r0 = t * bsz
            n_re = a_re * s_re - a_im * s_im + buf[r0:r0 + bsz, :S5_BLOCK_STATE]
            n_im = a_re * s_im + a_im * s_re + buf[r0:r0 + bsz, S5_BLOCK_STATE:]
            buf[r0:r0 + bsz, :S5_BLOCK_STATE] = n_re
            buf[r0:r0 + bsz, S5_BLOCK_STATE:] = n_im
            s_re, s_im = n_re, n_im
        s5_state[m, :, :S5_BLOCK_STATE] = s_re
        s5_state[m, :, S5_BLOCK_STATE:] = s_im
        y_parts.append(_dot(buf[...].astype(BF16), wc_ref[m]))
    y = jnp.concatenate(y_parts, axis=-1) + d_ref[...] * s5_in
    z = jax.nn.gelu(y)
    y_s5 = z * jax.nn.sigmoid(_dot(z.astype(BF16), glu_w_ref[...]) + glu_b_ref[...])

    a_low = proj(C_ALOW, A_LOW_PAD).astype(BF16)
    loga_buf[...] = jax.nn.log_sigmoid(_dot(a_low, a_up_w_ref[...]) + a_up_b_ref[...]) / GLA_TAU

    b_last = jnp.zeros((bsz, GLA_KEY_WIDTH), F32)
    for t in range(chunk):
        b_last = b_last + loga_buf[t * bsz:(t + 1) * bsz, :]
        bcum_buf[t * bsz:(t + 1) * bsz, :] = b_last
    bcum = bcum_buf[...]
    b_last_rows = jnp.broadcast_to(b_last[None], (chunk, bsz, GLA_KEY_WIDTH)).reshape(
        rows, GLA_KEY_WIDTH)
    q = proj(C_Q, GLA_KEY_WIDTH) * (GLA_DK ** -0.5)
    k = proj(C_K, GLA_KEY_WIDTH)
    q_t = q * jnp.exp(bcum)
    k_t = (k * jnp.exp(-bcum)).astype(BF16)
    k_end = k * jnp.exp(b_last_rows - bcum)
    v = proj(C_V, GLA_VAL_WIDTH).astype(BF16)

    row = lax.broadcasted_iota(jnp.int32, (rows, rows), 0)
    col = lax.broadcasted_iota(jnp.int32, (rows, rows), 1)
    causal = ((row % bsz) == (col % bsz)) & (col <= row)
    own_batch = (col // GLA_DK) == (row % bsz)
    own_batch8 = own_batch[:bsz]
    key_lane = lax.broadcasted_iota(jnp.int32, (rows, GLA_KEY_WIDTH), 1) // GLA_DK

    def expand(xh):
        rep = xh + pltpu.roll(xh, 2 * GLA_DK, 1)
        rep = rep + pltpu.roll(rep, GLA_DK, 1)
        return jnp.concatenate([rep, rep], axis=-1)

    o_parts = []
    for hd in range(GLA_HEADS):
        head = key_lane == hd
        q_h = jnp.where(head, q_t, 0.0)
        scores = _dot_nt(q_h.astype(BF16), k_t)
        p = jnp.where(causal, scores, 0.0).astype(BF16)
        v_h = v[:, hd * GLA_DV:(hd + 1) * GLA_DV]
        o_h = _dot(p, v_h)
        q_exp = jnp.where(own_batch, expand(q_h), 0.0).astype(BF16)
        s_prev = gla_state[hd]
        o_h = o_h + _dot_nt(q_exp, s_prev.astype(BF16))
        k_exp = jnp.where(own_batch, expand(jnp.where(head, k_end, 0.0)), 0.0).astype(BF16)
        d_state = _dot_tn(v_h, k_exp)
        bl = expand(jnp.where(head[:bsz], b_last, 0.0))
        decay = jnp.exp(jnp.sum(jnp.where(own_batch8, bl, 0.0), axis=0, keepdims=True))
        gla_state[hd] = decay * s_prev + d_state
        o_parts.append(o_h * lax.rsqrt(jnp.mean(o_h * o_h, axis=-1, keepdims=True) + EPS))
    o = jnp.concatenate(o_parts, axis=-1) * gnorm_ref[...]
    r = proj(C_R, GLA_VAL_WIDTH)
    y_gla = o * (r * jax.nn.sigmoid(r))

    merged = (jax.nn.sigmoid(proj(C_GS5, D_MODEL)) * _dot(y_s5.astype(BF16), p_s5_ref[...])
              + jax.nn.sigmoid(proj(C_GGLA, D_MODEL)) * _dot(y_gla.astype(BF16), p_gla_ref[...]))
    out = h + _dot(merged.astype(BF16), w_out_ref[...])
    o_ref[...] = out.reshape(chunk, bsz, D_MODEL)


def _mixer_call(h, g, w_in, a_re, a_im, wb, wc, d_skip, glu_w, glu_b, a_up_w, a_up_b, gnorm,
                p_s5, p_gla, w_out, *, bsz, seq):
    chunk = GLA_CHUNK
    rows = bsz * chunk
    args = (g, w_in, a_re, a_im, wb, wc, d_skip, glu_w, glu_b, a_up_w, a_up_b, gnorm,
            p_s5, p_gla, w_out)
    tile_spec = pl.BlockSpec((chunk, bsz, D_MODEL), lambda i: (i, 0, 0))
    return pl.pallas_call(
        functools.partial(_mixer_body, bsz=bsz, chunk=chunk),
        grid=(seq // chunk,),
        in_specs=[tile_spec] + [_const_spec(a.shape) for a in args],
        out_specs=tile_spec,
        out_shape=jax.ShapeDtypeStruct((seq, bsz, D_MODEL), F32),
        scratch_shapes=[
            pltpu.VMEM((S5_BLOCKS, bsz, 2 * S5_BLOCK_STATE), F32),
            pltpu.VMEM((GLA_HEADS, GLA_DV, bsz * GLA_DK), F32),
            pltpu.VMEM((S5_BLOCKS, rows, 2 * S5_BLOCK_STATE), F32),
            pltpu.VMEM((rows, GLA_KEY_WIDTH), F32),
            pltpu.VMEM((rows, GLA_KEY_WIDTH), F32),
        ],
        compiler_params=pltpu.CompilerParams(
            dimension_semantics=("arbitrary",), vmem_limit_bytes=VMEM_LIMIT),
        name="mixer",
    )(h, *args)


def _s5_discretize(lam_re, lam_im, log_dt, b_re, b_im):
    dt = jnp.exp(log_dt)[:, None]
    mag = jnp.exp(lam_re * dt)
    ar = mag * jnp.cos(lam_im * dt)
    ai = mag * jnp.sin(lam_im * dt)
    den = lam_re * lam_re + lam_im * lam_im
    nr = ar - 1.0
    fr = (nr * lam_re + ai * lam_im) / den
    fi = (ai * lam_re - nr * lam_im) / den
    bbar_re = fr[:, :, None] * b_re - fi[:, :, None] * b_im
    bbar_im = fr[:, :, None] * b_im + fi[:, :, None] * b_re
    return ar, ai, bbar_re, bbar_im


def _pack_s5(ar, ai, bbar_re, bbar_im, c_re, c_im):
    eye = jnp.eye(S5_BLOCK_GROUPS, dtype=F32)
    nb, ng, ns, nh = S5_BLOCKS, S5_BLOCK_GROUPS, S5_STATE, S5_GROUP
    bb = jnp.stack([bbar_re, bbar_im]).reshape(2, nb, ng, ns, nh)
    wb = jnp.einsum('rmgph,gk->mghrkp', bb, eye).reshape(nb, ng * nh, 2 * ng * ns)
    cc = jnp.stack([c_re, -c_im]).reshape(2, nb, ng, nh, ns)
    wc = jnp.einsum('rmghp,gk->mrgpkh', cc, eye).reshape(nb, 2 * ng * ns, ng * nh)
    a_re = ar.reshape(nb, 1, ng * ns)
    a_im = ai.reshape(nb, 1, ng * ns)
    return a_re, a_im, wb.astype(BF16), wc.astype(BF16)


def kernel(x, ffn1_norm, ffn1_w1, ffn1_w3, ffn1_w2, mix_norm, w_in, s5_lambda_re, s5_lambda_im, s5_log_dt, s5_b_re, s5_b_im, s5_c_re, s5_c_im, s5_d, s5_glu_w, s5_glu_b, gla_a_up_w, gla_a_up_b, gla_out_norm, proj_s5, proj_gla, w_out, ffn2_norm, ffn2_w1, ffn2_w3, ffn2_w2, final_norm):
    bsz, seq, _ = x.shape
    assert bsz == SUBLANES and seq % GLA_CHUNK == 0 and ffn1_norm.shape[0] == 1
    l = 0
    row = lambda a: a.reshape(1, -1).astype(F32)

    h = _ffn_call(x, row(ffn1_norm[l]), ffn1_w1[l].astype(BF16), ffn1_w3[l].astype(BF16),
                  ffn1_w2[l].astype(BF16), row(final_norm),
                  batch_major_in=True, final_norm=False, bsz=bsz, seq=seq)

    offs = [0]
    for s in IN_SIZES:
        offs.append(offs[-1] + s)
    seg = [w_in[l][:, offs[i]:offs[i + 1]] for i in range(len(IN_SIZES))]
    w_in_p = jnp.concatenate(
        seg[:5] + seg[6:] + [seg[5], jnp.zeros((D_MODEL, A_LOW_PAD - GLA_GATE_RANK), F32)],
        axis=1).astype(BF16)
    a_up_w = jnp.concatenate(
        [gla_a_up_w[l], jnp.zeros((A_LOW_PAD - GLA_GATE_RANK, GLA_KEY_WIDTH), F32)],
        axis=0).astype(BF16)
    ar, ai, bbar_re, bbar_im = _s5_discretize(
        s5_lambda_re[l], s5_lambda_im[l], s5_log_dt[l], s5_b_re[l], s5_b_im[l])
    a_re, a_im, wb, wc = _pack_s5(ar, ai, bbar_re, bbar_im, s5_c_re[l], s5_c_im[l])

    h = _mixer_call(h, row(mix_norm[l]), w_in_p, a_re, a_im, wb, wc, row(s5_d[l]),
                    s5_glu_w[l].astype(BF16), row(s5_glu_b[l]), a_up_w, row(gla_a_up_b[l]),
                    row(gla_out_norm[l]), proj_s5[l].astype(BF16), proj_gla[l].astype(BF16),
                    w_out[l].astype(BF16), bsz=bsz, seq=seq)

    return _ffn_call(h, row(ffn2_norm[l]), ffn2_w1[l].astype(BF16), ffn2_w3[l].astype(BF16),
                     ffn2_w2[l].astype(BF16), row(final_norm),
                     batch_major_in=False, final_norm=True, bsz=bsz, seq=seq)
```

```python
import functools
import math

import jax
import jax.numpy as jnp
from jax import lax
from jax.experimental import pallas as pl
from jax.experimental.pallas import tpu as pltpu

D_MODEL = 1024
S5_GROUP = 16
S5_WIDTH = 512
S5_GROUPS = 32
S5_STATE = 64
GLA_HEADS = 4
GLA_DV = 128
GLA_DK = 64
GLA_KEY_WIDTH = 256
GLA_VAL_WIDTH = 512
GLA_GATE_RANK = 16
GLA_TAU = 16.0
GLA_CHUNK = 64
D_FF = 2816
EPS = 1e-6
IN_SIZES = (S5_WIDTH, GLA_KEY_WIDTH, GLA_KEY_WIDTH, GLA_VAL_WIDTH, GLA_VAL_WIDTH,
            GLA_GATE_RANK, D_MODEL, D_MODEL)

LANES = 128
SUBLANES = 8
S5_BLOCKS = 4
S5_BLOCK_GROUPS = S5_GROUPS // S5_BLOCKS
S5_BLOCK_STATE = S5_BLOCK_GROUPS * S5_STATE
A_LOW_PAD = LANES
SLABS = D_MODEL // LANES
FFN_STEPS = 128
FF_CHUNKS = (768, 768, 768, 512)
assert sum(FF_CHUNKS) == D_FF
VMEM_LIMIT = 56 * 1024 * 1024

BF16 = jnp.bfloat16
F32 = jnp.float32

C_S5 = 0
C_Q = C_S5 + S5_WIDTH
C_K = C_Q + GLA_KEY_WIDTH
C_V = C_K + GLA_KEY_WIDTH
C_R = C_V + GLA_VAL_WIDTH
C_GS5 = C_R + GLA_VAL_WIDTH
C_GGLA = C_GS5 + D_MODEL
C_ALOW = C_GGLA + D_MODEL
IN_COLS_PAD = C_ALOW + A_LOW_PAD


def _dot(a, b):
    return jnp.dot(a, b, preferred_element_type=F32)


def _dot_nt(a, b):
    return lax.dot_general(a, b, (((1,), (1,)), ((), ())), preferred_element_type=F32)


def _dot_tn(a, b):
    return lax.dot_general(a, b, (((0,), (0,)), ((), ())), preferred_element_type=F32)


def _rms_norm(x, g):
    return x * lax.rsqrt(jnp.mean(x * x, axis=-1, keepdims=True) + EPS) * g


def _const_spec(shape):
    nd = len(shape)
    return pl.BlockSpec(shape, lambda *_: (0,) * nd, pipeline_mode=pl.Buffered(1))


def _ffn_body(x_ref, g_ref, w1_ref, w3_ref, w2_ref, gf_ref, o_ref, *,
              batch_major_in, final_norm, bsz, steps):
    rows = bsz * steps
    if batch_major_in:
        x = x_ref[...].reshape(rows, D_MODEL)
    else:
        x = jnp.concatenate(
            [jnp.concatenate([x_ref[j, pl.ds(bi, steps, stride=bsz), :] for j in range(SLABS)],
                             axis=-1) for bi in range(bsz)], axis=0)
    xn = _rms_norm(x, g_ref[...]).astype(BF16)
    acc = None
    col = 0
    for width in FF_CHUNKS:
        a = _dot(xn, w1_ref[:, col:col + width])
        b = _dot(xn, w3_ref[:, col:col + width])
        hmid = (a * jax.nn.sigmoid(a) * b).astype(BF16)
        part = _dot(hmid, w2_ref[col:col + width, :])
        acc = part if acc is None else acc + part
        col += width
    y = x + 0.5 * acc
    if final_norm:
        y = _rms_norm(y, gf_ref[...])
    if batch_major_in:
        for bi in range(bsz):
            for j in range(SLABS):
                o_ref[j, pl.ds(bi, steps, stride=bsz), :] = (
                    y[bi * steps:(bi + 1) * steps, j * LANES:(j + 1) * LANES])
    else:
        o_ref[...] = y.reshape(bsz, steps, D_MODEL)


def _ffn_call(x, g, w1, w3, w2, gf, *, batch_major_in, final_norm, bsz, seq):
    steps = min(FFN_STEPS, seq)
    bm_spec = pl.BlockSpec((bsz, steps, D_MODEL), lambda i: (0, i, 0))
    tm_spec = pl.BlockSpec((SLABS, steps * bsz, LANES), lambda i: (0, i, 0))
    in_spec, out_spec = (bm_spec, tm_spec) if batch_major_in else (tm_spec, bm_spec)
    out_shape = (SLABS, seq * bsz, LANES) if batch_major_in else (bsz, seq, D_MODEL)
    return pl.pallas_call(
        functools.partial(_ffn_body, batch_major_in=batch_major_in, final_norm=final_norm,
                          bsz=bsz, steps=steps),
        grid=(seq // steps,),
        in_specs=[in_spec, _const_spec((1, D_MODEL)), _const_spec((D_MODEL, D_FF)),
                  _const_spec((D_MODEL, D_FF)), _const_spec((D_FF, D_MODEL)),
                  _const_spec((1, D_MODEL))],
        out_specs=out_spec,
        out_shape=jax.ShapeDtypeStruct(out_shape, F32),
        compiler_params=pltpu.CompilerParams(
            dimension_semantics=("arbitrary",), vmem_limit_bytes=VMEM_LIMIT),
        name="ffn_final" if final_norm else "ffn",
    )(x, g, w1, w3, w2, gf)


def _mixer_body(h_ref, g_ref, w_in_ref, a_re_ref, a_im_ref, wb_ref, wc_ref, d_ref,
                glu_w_ref, glu_b_ref, a_up_w_ref, a_up_b_ref, gnorm_ref,
                p_s5_ref, p_gla_ref, w_out_ref, o_ref,
                s5_state, gla_state, bu_buf, loga_buf, bcum_buf, *, bsz, chunk):
    rows = bsz * chunk

    @pl.when(pl.program_id(0) == 0)
    def _():
        s5_state[...] = jnp.zeros_like(s5_state)
        gla_state[...] = jnp.zeros_like(gla_state)

    h = jnp.concatenate([h_ref[j] for j in range(SLABS)], axis=-1)
    u = _rms_norm(h, g_ref[...]).astype(BF16)

    def proj(col, width):
        return _dot(u, w_in_ref[:, col:col + width])

    s5_in = proj(C_S5, S5_WIDTH)
    s5_in_bf = s5_in.astype(BF16)
    y_parts = []
    for m in range(S5_BLOCKS):
        buf = bu_buf.at[m]
        buf[...] = _dot(s5_in_bf[:, m * LANES:(m + 1) * LANES], wb_ref[m])
        a_re = jnp.broadcast_to(a_re_ref[m], (bsz, S5_BLOCK_STATE))
        a_im = jnp.broadcast_to(a_im_ref[m], (bsz, S5_BLOCK_STATE))
        s_re = s5_state[m, :, :S5_BLOCK_STATE]
        s_im = s5_state[m, :, S5_BLOCK_STATE:]
        for t in range(chunk):
            r0 = t * bsz
            n_re = a_re * s_re - a_im * s_im + buf[r0:r0 + bsz, :S5_BLOCK_STATE]
            n_im = a_re * s_im + a_im * s_re + buf[r0:r0 + bsz, S5_BLOCK_STATE:]
            buf[r0:r0 + bsz, :S5_BLOCK_STATE] = n_re
            buf[r0:r0 + bsz, S5_BLOCK_STATE:] = n_im
            s_re, s_im = n_re, n_im
        s5_state[m, :, :S5_BLOCK_STATE] = s_re
        s5_state[m, :, S5_BLOCK_STATE:] = s_im
        y_parts.append(_dot(buf[...].astype(BF16), wc_ref[m]))
    y = jnp.concatenate(y_parts, axis=-1) + d_ref[...] * s5_in
    z = jax.nn.gelu(y)
    y_s5 = z * jax.nn.sigmoid(_dot(z.astype(BF16), glu_w_ref[...]) + glu_b_ref[...])

    a_low = proj(C_ALOW, A_LOW_PAD).astype(BF16)
    loga_buf[...] = jax.nn.log_sigmoid(_dot(a_low, a_up_w_ref[...]) + a_up_b_ref[...]) / GLA_TAU

    b_last = jnp.zeros((bsz, GLA_KEY_WIDTH), F32)
    for t in range(chunk):
        b_last = b_last + loga_buf[t * bsz:(t + 1) * bsz, :]
        bcum_buf[t * bsz:(t + 1) * bsz, :] = b_last
    bcum = bcum_buf[...]
    b_last_rows = jnp.broadcast_to(b_last[None], (chunk, bsz, GLA_KEY_WIDTH)).reshape(
        rows, GLA_KEY_WIDTH)
    q = proj(C_Q, GLA_KEY_WIDTH) * (GLA_DK ** -0.5)
    k = proj(C_K, GLA_KEY_WIDTH)
    q_t = q * jnp.exp(bcum)
    k_t = (k * jnp.exp(-bcum)).astype(BF16)
    k_end = k * jnp.exp(b_last_rows - bcum)
    v = proj(C_V, GLA_VAL_WIDTH).astype(BF16)

    row = lax.broadcasted_iota(jnp.int32, (rows, rows), 0)
    col = lax.broadcasted_iota(jnp.int32, (rows, rows), 1)
    causal = ((row % bsz) == (col % bsz)) & (col <= row)
    own_batch = (col // GLA_DK) == (row % bsz)
    own_batch8 = own_batch[:bsz]
    key_lane = lax.broadcasted_iota(jnp.int32, (rows, GLA_KEY_WIDTH), 1) // GLA_DK

    def expand(xh):
        rep = xh + pltpu.roll(xh, 2 * GLA_DK, 1)
        rep = rep + pltpu.roll(rep, GLA_DK, 1)
        return jnp.concatenate([rep, rep], axis=-1)

    o_parts = []
    for hd in range(GLA_HEADS):
        head = key_lane == hd
        q_h = jnp.where(head, q_t, 0.0)
        scores = _dot_nt(q_h.astype(BF16), k_t)
        p = jnp.where(causal, scores, 0.0).astype(BF16)
        v_h = v[:, hd * GLA_DV:(hd + 1) * GLA_DV]
        o_h = _dot(p, v_h)
        q_exp = jnp.where(own_batch, expand(q_h), 0.0).astype(BF16)
        s_prev = gla_state[hd]
        o_h = o_h + _dot_nt(q_exp, s_prev.astype(BF16))
        k_exp = jnp.where(own_batch, expand(jnp.where(head, k_end, 0.0)), 0.0).astype(BF16)
        d_state = _dot_tn(v_h, k_exp)
        bl = expand(jnp.where(head[:bsz], b_last, 0.0))
        decay = jnp.exp(jnp.sum(jnp.where(own_batch8, bl, 0.0), axis=0, keepdims=True))
        gla_state[hd] = decay * s_prev + d_state
        o_parts.append(o_h * lax.rsqrt(jnp.mean(o_h * o_h, axis=-1, keepdims=True) + EPS))
    o = jnp.concatenate(o_parts, axis=-1) * gnorm_ref[...]
    r = proj(C_R, GLA_VAL_WIDTH)
    y_gla = o * (r * jax.nn.sigmoid(r))

    merged = (jax.nn.sigmoid(proj(C_GS5, D_MODEL)) * _dot(y_s5.astype(BF16), p_s5_ref[...])
              + jax.nn.sigmoid(proj(C_GGLA, D_MODEL)) * _dot(y_gla.astype(BF16), p_gla_ref[...]))
    out = h + _dot(merged.astype(BF16), w_out_ref[...])
    for j in range(SLABS):
        o_ref[j] = out[:, j * LANES:(j + 1) * LANES]


def _mixer_call(h, g, w_in, a_re, a_im, wb, wc, d_skip, glu_w, glu_b, a_up_w, a_up_b, gnorm,
                p_s5, p_gla, w_out, *, bsz, seq):
    chunk = GLA_CHUNK
    rows = bsz * chunk
    args = (g, w_in, a_re, a_im, wb, wc, d_skip, glu_w, glu_b, a_up_w, a_up_b, gnorm,
            p_s5, p_gla, w_out)
    tile_spec = pl.BlockSpec((SLABS, rows, LANES), lambda i: (0, i, 0))
    return pl.pallas_call(
        functools.partial(_mixer_body, bsz=bsz, chunk=chunk),
        grid=(seq // chunk,),
        in_specs=[tile_spec] + [_const_spec(a.shape) for a in args],
        out_specs=tile_spec,
        out_shape=jax.ShapeDtypeStruct((SLABS, seq * bsz, LANES), F32),
        scratch_shapes=[
            pltpu.VMEM((S5_BLOCKS, bsz, 2 * S5_BLOCK_STATE), F32),
            pltpu.VMEM((GLA_HEADS, GLA_DV, bsz * GLA_DK), F32),
            pltpu.VMEM((S5_BLOCKS, rows, 2 * S5_BLOCK_STATE), F32),
            pltpu.VMEM((rows, GLA_KEY_WIDTH), F32),
            pltpu.VMEM((rows, GLA_KEY_WIDTH), F32),
        ],
        compiler_params=pltpu.CompilerParams(
            dimension_semantics=("arbitrary",), vmem_limit_bytes=VMEM_LIMIT),
        name="mixer",
    )(h, *args)


def _s5_discretize(lam_re, lam_im, log_dt, b_re, b_im):
    dt = jnp.exp(log_dt)[:, None]
    mag = jnp.exp(lam_re * dt)
    ar = mag * jnp.cos(lam_im * dt)
    ai = mag * jnp.sin(lam_im * dt)
    den = lam_re * lam_re + lam_im * lam_im
    nr = ar - 1.0
    fr = (nr * lam_re + ai * lam_im) / den
    fi = (ai * lam_re - nr * lam_im) / den
    bbar_re = fr[:, :, None] * b_re - fi[:, :, None] * b_im
    bbar_im = fr[:, :, None] * b_im + fi[:, :, None] * b_re
    return ar, ai, bbar_re, bbar_im


def _pack_s5(ar, ai, bbar_re, bbar_im, c_re, c_im):
    eye = jnp.eye(S5_BLOCK_GROUPS, dtype=F32)
    nb, ng, ns, nh = S5_BLOCKS, S5_BLOCK_GROUPS, S5_STATE, S5_GROUP
    bb = jnp.stack([bbar_re, bbar_im]).reshape(2, nb, ng, ns, nh)
    wb = jnp.einsum('rmgph,gk->mghrkp', bb, eye).reshape(nb, ng * nh, 2 * ng * ns)
    cc = jnp.stack([c_re, -c_im]).reshape(2, nb, ng, nh, ns)
    wc = jnp.einsum('rmghp,gk->mrgpkh', cc, eye).reshape(nb, 2 * ng * ns, ng * nh)
    a_re = ar.reshape(nb, 1, ng * ns)
    a_im = ai.reshape(nb, 1, ng * ns)
    return a_re, a_im, wb.astype(BF16), wc.astype(BF16)


def kernel(x, ffn1_norm, ffn1_w1, ffn1_w3, ffn1_w2, mix_norm, w_in, s5_lambda_re, s5_lambda_im, s5_log_dt, s5_b_re, s5_b_im, s5_c_re, s5_c_im, s5_d, s5_glu_w, s5_glu_b, gla_a_up_w, gla_a_up_b, gla_out_norm, proj_s5, proj_gla, w_out, ffn2_norm, ffn2_w1, ffn2_w3, ffn2_w2, final_norm):
    bsz, seq, _ = x.shape
    assert bsz == SUBLANES and seq % GLA_CHUNK == 0 and ffn1_norm.shape[0] == 1
    l = 0
    row = lambda a: a.reshape(1, -1).astype(F32)

    h = _ffn_call(x, row(ffn1_norm[l]), ffn1_w1[l].astype(BF16), ffn1_w3[l].astype(BF16),
                  ffn1_w2[l].astype(BF16), row(final_norm),
                  batch_major_in=True, final_norm=False, bsz=bsz, seq=seq)

    offs = [0]
    for s in IN_SIZES:
        offs.append(offs[-1] + s)
    seg = [w_in[l][:, offs[i]:offs[i + 1]] for i in range(len(IN_SIZES))]
    w_in_p = jnp.concatenate(
        seg[:5] + seg[6:] + [seg[5], jnp.zeros((D_MODEL, A_LOW_PAD - GLA_GATE_RANK), F32)],
        axis=1).astype(BF16)
    a_up_w = jnp.concatenate(
        [gla_a_up_w[l], jnp.zeros((A_LOW_PAD - GLA_GATE_RANK, GLA_KEY_WIDTH), F32)],
        axis=0).astype(BF16)
    ar, ai, bbar_re, bbar_im = _s5_discretize(
        s5_lambda_re[l], s5_lambda_im[l], s5_log_dt[l], s5_b_re[l], s5_b_im[l])
    a_re, a_im, wb, wc = _pack_s5(ar, ai, bbar_re, bbar_im, s5_c_re[l], s5_c_im[l])

    h = _mixer_call(h, row(mix_norm[l]), w_in_p, a_re, a_im, wb, wc, row(s5_d[l]),
                    s5_glu_w[l].astype(BF16), row(s5_glu_b[l]), a_up_w, row(gla_a_up_b[l]),
                    row(gla_out_norm[l]), proj_s5[l].astype(BF16), proj_gla[l].astype(BF16),
                    w_out[l].astype(BF16), bsz=bsz, seq=seq)

    return _ffn_call(h, row(ffn2_norm[l]), ffn2_w1[l].astype(BF16), ffn2_w3[l].astype(BF16),
                     ffn2_w2[l].astype(BF16), row(final_norm),
                     batch_major_in=False, final_norm=True, bsz=bsz, seq=seq)
```

```python
import functools
import math

import jax
import jax.numpy as jnp
from jax import lax
from jax.experimental import pallas as pl
from jax.experimental.pallas import tpu as pltpu

D_MODEL = 1024
S5_GROUP = 16
S5_WIDTH = 512
S5_GROUPS = 32
S5_STATE = 64
GLA_HEADS = 4
GLA_DV = 128
GLA_DK = 64
GLA_KEY_WIDTH = 256
GLA_VAL_WIDTH = 512
GLA_GATE_RANK = 16
GLA_TAU = 16.0
GLA_CHUNK = 64
D_FF = 2816
EPS = 1e-6
IN_SIZES = (S5_WIDTH, GLA_KEY_WIDTH, GLA_KEY_WIDTH, GLA_VAL_WIDTH, GLA_VAL_WIDTH,
            GLA_GATE_RANK, D_MODEL, D_MODEL)

LANES = 128
SUBLANES = 8
S5_BLOCKS = 4
S5_BLOCK_GROUPS = S5_GROUPS // S5_BLOCKS
S5_BLOCK_STATE = S5_BLOCK_GROUPS * S5_STATE
A_LOW_PAD = LANES
SLABS = D_MODEL // LANES
FFN_STEPS = 128
FF_CHUNKS = (768, 768, 768, 512)
assert sum(FF_CHUNKS) == D_FF
VMEM_LIMIT = 56 * 1024 * 1024

BF16 = jnp.bfloat16
F32 = jnp.float32

C_S5 = 0
C_Q = C_S5 + S5_WIDTH
C_K = C_Q + GLA_KEY_WIDTH
C_V = C_K + GLA_KEY_WIDTH
C_R = C_V + GLA_VAL_WIDTH
C_GS5 = C_R + GLA_VAL_WIDTH
C_GGLA = C_GS5 + D_MODEL
C_ALOW = C_GGLA + D_MODEL
IN_COLS_PAD = C_ALOW + A_LOW_PAD


def _dot(a, b):
    return jnp.dot(a, b, preferred_element_type=F32)


def _dot_nt(a, b):
    return lax.dot_general(a, b, (((1,), (1,)), ((), ())), preferred_element_type=F32)


def _dot_tn(a, b):
    return lax.dot_general(a, b, (((0,), (0,)), ((), ())), preferred_element_type=F32)


def _rms_norm(x, g):
    return x * lax.rsqrt(jnp.mean(x * x, axis=-1, keepdims=True) + EPS) * g


def _const_spec(shape):
    nd = len(shape)
    return pl.BlockSpec(shape, lambda *_: (0,) * nd, pipeline_mode=pl.Buffered(1))


def _ffn_body(x_ref, g_ref, w1_ref, w3_ref, w2_ref, gf_ref, o_ref, *,
              batch_major_in, final_norm, bsz, steps):
    rows = bsz * steps
    if batch_major_in:
        x = x_ref[...].reshape(rows, D_MODEL)
    else:
        x = jnp.concatenate(
            [jnp.concatenate([x_ref[j, pl.ds(bi, steps, stride=bsz), :] for j in range(SLABS)],
                             axis=-1) for bi in range(bsz)], axis=0)
    xn = _rms_norm(x, g_ref[...]).astype(BF16)
    acc = None
    col = 0
    for width in FF_CHUNKS:
        a = _dot(xn, w1_ref[:, col:col + width])
        b = _dot(xn, w3_ref[:, col:col + width])
        hmid = (a * jax.nn.sigmoid(a) * b).astype(BF16)
        part = _dot(hmid, w2_ref[col:col + width, :])
        acc = part if acc is None else acc + part
        col += width
    y = x + 0.5 * acc
    if final_norm:
        y = _rms_norm(y, gf_ref[...])
    if batch_major_in:
        for bi in range(bsz):
            for j in range(SLABS):
                o_ref[j, pl.ds(bi, steps, stride=bsz), :] = (
                    y[bi * steps:(bi + 1) * steps, j * LANES:(j + 1) * LANES])
    else:
        o_ref[...] = y.reshape(bsz, steps, D_MODEL)


def _ffn_call(x, g, w1, w3, w2, gf, *, batch_major_in, final_norm, bsz, seq):
    steps = min(FFN_STEPS, seq)
    bm_spec = pl.BlockSpec((bsz, steps, D_MODEL), lambda i: (0, i, 0))
    tm_spec = pl.BlockSpec((SLABS, steps * bsz, LANES), lambda i: (0, i, 0))
    in_spec, out_spec = (bm_spec, tm_spec) if batch_major_in else (tm_spec, bm_spec)
    out_shape = (SLABS, seq * bsz, LANES) if batch_major_in else (bsz, seq, D_MODEL)
    return pl.pallas_call(
        functools.partial(_ffn_body, batch_major_in=batch_major_in, final_norm=final_norm,
                          bsz=bsz, steps=steps),
        grid=(seq // steps,),
        in_specs=[in_spec, _const_spec((1, D_MODEL)), _const_spec((D_MODEL, D_FF)),
                  _const_spec((D_MODEL, D_FF)), _const_spec((D_FF, D_MODEL)),
                  _const_spec((1, D_MODEL))],
        out_specs=out_spec,
        out_shape=jax.ShapeDtypeStruct(out_shape, F32),
        compiler_params=pltpu.CompilerParams(
            dimension_semantics=("arbitrary",), vmem_limit_bytes=VMEM_LIMIT),
        name="ffn_final" if final_norm else "ffn",
    )(x, g, w1, w3, w2, gf)


def _mixer_body(h_ref, g_ref, w_in_ref, a_re_ref, a_im_ref, wb_ref, wc_ref, d_ref,
                glu_w_ref, glu_b_ref, a_up_w_ref, a_up_b_ref, gnorm_ref,
                p_s5_ref, p_gla_ref, w_out_ref, o_ref,
                s5_state, gla_state, bu_buf, loga_buf, bcum_buf, *, bsz, chunk):
    rows = bsz * chunk

    @pl.when(pl.program_id(0) == 0)
    def _():
        s5_state[...] = jnp.zeros_like(s5_state)
        gla_state[...] = jnp.zeros_like(gla_state)

    h = jnp.concatenate([h_ref[j] for j in range(SLABS)], axis=-1)
    u = _rms_norm(h, g_ref[...]).astype(BF16)

    def proj(col, width):
        return _dot(u, w_in_ref[:, col:col + width])

    s5_in = proj(C_S5, S5_WIDTH)
    s5_in_bf = s5_in.astype(BF16)
    y_parts = []
    for m in range(S5_BLOCKS):
        buf = bu_buf.at[m]
        buf[...] = _dot(s5_in_bf[:, m * LANES:(m + 1) * LANES], wb_ref[m])
        a_re = jnp.broadcast_to(a_re_ref[m], (bsz, S5_BLOCK_STATE))
        a_im = jnp.broadcast_to(a_im_ref[m], (bsz, S5_BLOCK_STATE))
        s_re = s5_state[m, :, :S5_BLOCK_STATE]
        s_im = s5_state[m, :, S5_BLOCK_STATE:]
        for t in range(chunk):
            r0 = t * bsz
            n_re = a_re * s_re - a_im * s_im + buf[r0:r0 + bsz, :S5_BLOCK_STATE]
            n_im = a_re * s_im + a_im * s_re + buf[r0:r0 + bsz, S5_BLOCK_STATE:]
            buf[r0:r0 + bsz, :S5_BLOCK_STATE] = n_re
            buf[r0:r0 + bsz, S5_BLOCK_STATE:] = n_im
            s_re, s_im = n_re, n_im
        s5_state[m, :, :S5_BLOCK_STATE] = s_re
        s5_state[m, :, S5_BLOCK_STATE:] = s_im
        y_parts.append(_dot(buf[...].astype(BF16), wc_ref[m]))
    y = jnp.concatenate(y_parts, axis=-1) + d_ref[...] * s5_in
    z = jax.nn.gelu(y)
    y_s5 = z * jax.nn.sigmoid(_dot(z.astype(BF16), glu_w_ref[...]) + glu_b_ref[...])

    a_low = proj(C_ALOW, A_LOW_PAD).astype(BF16)
    loga_buf[...] = jax.nn.log_sigmoid(_dot(a_low, a_up_w_ref[...]) + a_up_b_ref[...]) / GLA_TAU

    b_last = jnp.zeros((bsz, GLA_KEY_WIDTH), F32)
    for t in range(chunk):
        b_last = b_last + loga_buf[t * bsz:(t + 1) * bsz, :]
        bcum_buf[t * bsz:(t + 1) * bsz, :] = b_last
    bcum = bcum_buf[...]
    b_last_rows = jnp.broadcast_to(b_last[None], (chunk, bsz, GLA_KEY_WIDTH)).reshape(
        rows, GLA_KEY_WIDTH)
    q = proj(C_Q, GLA_KEY_WIDTH) * (GLA_DK ** -0.5)
    k = proj(C_K, GLA_KEY_WIDTH)
    q_t = q * jnp.exp(bcum)
    k_t = (k * jnp.exp(-bcum)).astype(BF16)
    k_end = k * jnp.exp(b_last_rows - bcum)
    v = proj(C_V, GLA_VAL_WIDTH).astype(BF16)

    row = lax.broadcasted_iota(jnp.int32, (rows, rows), 0)
    col = lax.broadcasted_iota(jnp.int32, (rows, rows), 1)
    causal = ((row % bsz) == (col % bsz)) & (col <= row)
    own_batch = (col // GLA_DK) == (row % bsz)
    own_batch8 = own_batch[:bsz]
    key_lane = lax.broadcasted_iota(jnp.int32, (rows, GLA_KEY_WIDTH), 1) // GLA_DK

    def expand(xh):
        rep = xh + pltpu.roll(xh, 2 * GLA_DK, 1)
        rep = rep + pltpu.roll(rep, GLA_DK, 1)
        return jnp.concatenate([rep, rep], axis=-1)

    o_parts = []
    for hd in range(GLA_HEADS):
        head = key_lane == hd
        q_h = jnp.where(head, q_t, 0.0)
        scores = _dot_nt(q_h.astype(BF16), k_t)
        p = jnp.where(causal, scores, 0.0).astype(BF16)
        v_h = v[:, hd * GLA_DV:(hd + 1) * GLA_DV]
        o_h = _dot(p, v_h)
        q_exp = jnp.where(own_batch, expand(q_h), 0.0).astype(BF16)
        s_prev = gla_state[hd]
        o_h = o_h + _dot_nt(q_exp, s_prev.astype(BF16))
        k_exp = jnp.where(own_batch, expand(jnp.where(head, k_end, 0.0)), 0.0).astype(BF16)
        d_state = _dot_tn(v_h, k_exp)
        bl = expand(jnp.where(head[:bsz], b_last, 0.0))
        decay = jnp.exp(jnp.sum(jnp.where(own_batch8, bl, 0.0), axis=0, keepdims=True))
        gla_state[hd] = decay * s_prev + d_state
        o_parts.append(o_h * lax.rsqrt(jnp.mean(o_h * o_h, axis=-1, keepdims=True) + EPS))
    o = jnp.concatenate(o_parts, axis=-1) * gnorm_ref[...]
    r = proj(C_R, GLA_VAL_WIDTH)
    y_gla = o * (r * jax.nn.sigmoid(r))

    merged = (jax.nn.sigmoid(proj(C_GS5, D_MODEL)) * _dot(y_s5.astype(BF16), p_s5_ref[...])
              + jax.nn.sigmoid(proj(C_GGLA, D_MODEL)) * _dot(y_gla.astype(BF16), p_gla_ref[...]))
    out = h + _dot(merged.astype(BF16), w_out_ref[...])
    for j in range(SLABS):
        o_ref[j] = out[:, j * LANES:(j + 1) * LANES]


def _mixer_call(h, g, w_in, a_re, a_im, wb, wc, d_skip, glu_w, glu_b, a_up_w, a_up_b, gnorm,
                p_s5, p_gla, w_out, *, bsz, seq):
    chunk = GLA_CHUNK
    rows = bsz * chunk
    args = (g, w_in, a_re, a_im, wb, wc, d_skip, glu_w, glu_b, a_up_w, a_up_b, gnorm,
            p_s5, p_gla, w_out)
    tile_spec = pl.BlockSpec((SLABS, rows, LANES), lambda i: (0, i, 0))
    return pl.pallas_call(
        functools.partial(_mixer_body, bsz=bsz, chunk=chunk),
        grid=(seq // chunk,),
        in_specs=[tile_spec] + [_const_spec(a.shape) for a in args],
        out_specs=tile_spec,
        out_shape=jax.ShapeDtypeStruct((SLABS, seq * bsz, LANES), F32),
        scratch_shapes=[
            pltpu.VMEM((S5_BLOCKS, bsz, 2 * S5_BLOCK_STATE), F32),
            pltpu.VMEM((GLA_HEADS, GLA_DV, bsz * GLA_DK), F32),
            pltpu.VMEM((S5_BLOCKS, rows, 2 * S5_BLOCK_STATE), F32),
            pltpu.VMEM((rows, GLA_KEY_WIDTH), F32),
            pltpu.VMEM((rows, GLA_KEY_WIDTH), F32),
        ],
        compiler_params=pltpu.CompilerParams(
            dimension_semantics=("arbitrary",), vmem_limit_bytes=VMEM_LIMIT),
        name="mixer",
    )(h, *args)


def _s5_discretize_body(lam_re_ref, lam_im_ref, log_dt_ref, b_re_ref, b_im_ref,
                        ar_ref, ai_ref, bbar_re_ref, bbar_im_ref):
    lam_re = lam_re_ref[...]
    lam_im = lam_im_ref[...]
    dt = jnp.exp(log_dt_ref[...])
    mag = jnp.exp(lam_re * dt)
    ar = mag * jnp.cos(lam_im * dt)
    ai = mag * jnp.sin(lam_im * dt)
    den = lam_re * lam_re + lam_im * lam_im
    nr = ar - 1.0
    fr = ((nr * lam_re + ai * lam_im) / den)[:, None, :]
    fi = ((ai * lam_re - nr * lam_im) / den)[:, None, :]
    ar_ref[...] = ar
    ai_ref[...] = ai
    b_re = b_re_ref[...]
    b_im = b_im_ref[...]
    bbar_re_ref[...] = fr * b_re - fi * b_im
    bbar_im_ref[...] = fr * b_im + fi * b_re


def _s5_discretize(lam_re, lam_im, log_dt, b_re, b_im):
    gp = jax.ShapeDtypeStruct((S5_GROUPS, S5_STATE), F32)
    ghp = jax.ShapeDtypeStruct((S5_GROUPS, S5_GROUP, S5_STATE), F32)
    return pl.pallas_call(
        _s5_discretize_body, out_shape=(gp, gp, ghp, ghp), name="s5_discretize",
    )(lam_re, lam_im, log_dt.reshape(S5_GROUPS, 1),
      jnp.swapaxes(b_re, 1, 2), jnp.swapaxes(b_im, 1, 2))


def _pack_s5(ar, ai, bbar_re, bbar_im, c_re, c_im):
    eye = jnp.eye(S5_BLOCK_GROUPS, dtype=F32)
    nb, ng, ns, nh = S5_BLOCKS, S5_BLOCK_GROUPS, S5_STATE, S5_GROUP
    bb = jnp.stack([bbar_re, bbar_im]).reshape(2, nb, ng, nh, ns)
    wb = jnp.einsum('rmghp,gk->mghrkp', bb, eye).reshape(nb, ng * nh, 2 * ng * ns)
    cc = jnp.stack([c_re, -c_im]).reshape(2, nb, ng, nh, ns)
    wc = jnp.einsum('rmghp,gk->mrgpkh', cc, eye).reshape(nb, 2 * ng * ns, ng * nh)
    a_re = ar.reshape(nb, 1, ng * ns)
    a_im = ai.reshape(nb, 1, ng * ns)
    return a_re, a_im, wb.astype(BF16), wc.astype(BF16)


def kernel(x, ffn1_norm, ffn1_w1, ffn1_w3, ffn1_w2, mix_norm, w_in, s5_lambda_re, s5_lambda_im, s5_log_dt, s5_b_re, s5_b_im, s5_c_re, s5_c_im, s5_d, s5_glu_w, s5_glu_b, gla_a_up_w, gla_a_up_b, gla_out_norm, proj_s5, proj_gla, w_out, ffn2_norm, ffn2_w1, ffn2_w3, ffn2_w2, final_norm):
    bsz, seq, _ = x.shape
    assert bsz == SUBLANES and seq % GLA_CHUNK == 0 and ffn1_norm.shape[0] == 1
    l = 0
    row = lambda a: a.reshape(1, -1).astype(F32)

    h = _ffn_call(x, row(ffn1_norm[l]), ffn1_w1[l].astype(BF16), ffn1_w3[l].astype(BF16),
                  ffn1_w2[l].astype(BF16), row(final_norm),
                  batch_major_in=True, final_norm=False, bsz=bsz, seq=seq)

    offs = [0]
    for s in IN_SIZES:
        offs.append(offs[-1] + s)
    seg = [w_in[l][:, offs[i]:offs[i + 1]].astype(BF16) for i in range(len(IN_SIZES))]
    w_in_p = jnp.concatenate(
        seg[:5] + seg[6:] + [seg[5], jnp.zeros((D_MODEL, A_LOW_PAD - GLA_GATE_RANK), BF16)],
        axis=1)
    a_up_w = jnp.concatenate(
        [gla_a_up_w[l], jnp.zeros((A_LOW_PAD - GLA_GATE_RANK, GLA_KEY_WIDTH), F32)],
        axis=0).astype(BF16)
    ar, ai, bbar_re, bbar_im = _s5_discretize(
        s5_lambda_re[l], s5_lambda_im[l], s5_log_dt[l], s5_b_re[l], s5_b_im[l])
    a_re, a_im, wb, wc = _pack_s5(ar, ai, bbar_re, bbar_im, s5_c_re[l], s5_c_im[l])

    h = _mixer_call(h, row(mix_norm[l]), w_in_p, a_re, a_im, wb, wc, row(s5_d[l]),
                    s5_glu_w[l].astype(BF16), row(s5_glu_b[l]), a_up_w, row(gla_a_up_b[l]),
                    row(gla_out_norm[l]), proj_s5[l].astype(BF16), proj_gla[l].astype(BF16),
                    w_out[l].astype(BF16), bsz=bsz, seq=seq)

    return _ffn_call(h, row(ffn2_norm[l]), ffn2_w1[l].astype(BF16), ffn2_w3[l].astype(BF16),
                     ffn2_w2[l].astype(BF16), row(final_norm),
                     batch_major_in=False, final_norm=True, bsz=bsz, seq=seq)
```

```python
import functools
import math

import jax
import jax.numpy as jnp
from jax import lax
from jax.experimental import pallas as pl
from jax.experimental.pallas import tpu as pltpu

D_MODEL = 1024
S5_GROUP = 16
S5_WIDTH = 512
S5_GROUPS = 32
S5_STATE = 64
GLA_HEADS = 4
GLA_DV = 128
GLA_DK = 64
GLA_KEY_WIDTH = 256
GLA_VAL_WIDTH = 512
GLA_GATE_RANK = 16
GLA_TAU = 16.0
GLA_CHUNK = 64
D_FF = 2816
EPS = 1e-6
IN_SIZES = (S5_WIDTH, GLA_KEY_WIDTH, GLA_KEY_WIDTH, GLA_VAL_WIDTH, GLA_VAL_WIDTH,
            GLA_GATE_RANK, D_MODEL, D_MODEL)

LANES = 128
SUBLANES = 8
S5_BLOCKS = 4
S5_BLOCK_GROUPS = S5_GROUPS // S5_BLOCKS
S5_BLOCK_STATE = S5_BLOCK_GROUPS * S5_STATE
A_LOW_PAD = LANES
SLABS = D_MODEL // LANES
FFN_STEPS = 128
FF_CHUNKS = (768, 768, 768, 512)
assert sum(FF_CHUNKS) == D_FF
VMEM_LIMIT = 56 * 1024 * 1024

BF16 = jnp.bfloat16
F32 = jnp.float32

C_S5 = 0
C_Q = C_S5 + S5_WIDTH
C_K = C_Q + GLA_KEY_WIDTH
C_V = C_K + GLA_KEY_WIDTH
C_R = C_V + GLA_VAL_WIDTH
MIX_COLS = C_R + GLA_VAL_WIDTH
C_GS5 = 0
C_GGLA = C_GS5 + D_MODEL
GATE_COLS = C_GGLA + D_MODEL


def _dot(a, b):
    return jnp.dot(a, b, preferred_element_type=F32)


def _dot_nt(a, b):
    return lax.dot_general(a, b, (((1,), (1,)), ((), ())), preferred_element_type=F32)


def _dot_tn(a, b):
    return lax.dot_general(a, b, (((0,), (0,)), ((), ())), preferred_element_type=F32)


def _inv_rms(x):
    return lax.rsqrt(jnp.mean(x * x, axis=-1, keepdims=True) + EPS)


def _rms_norm(x, g):
    return x * _inv_rms(x) * g


def _const_spec(shape):
    nd = len(shape)
    return pl.BlockSpec(shape, lambda *_: (0,) * nd, pipeline_mode=pl.Buffered(1))


def _ffn_body(x_ref, w1_ref, w3_ref, w2_ref, gf_ref, o_ref, *,
              batch_major_in, final_norm, bsz, steps):
    rows = bsz * steps
    if batch_major_in:
        x = x_ref[...].reshape(rows, D_MODEL)
    else:
        x = jnp.concatenate(
            [jnp.concatenate([x_ref[j, pl.ds(bi, steps, stride=bsz), :] for j in range(SLABS)],
                             axis=-1) for bi in range(bsz)], axis=0)
    xb = x.astype(BF16)
    inv = _inv_rms(x)
    acc = None
    col = 0
    for width in FF_CHUNKS:
        a = _dot(xb, w1_ref[:, col:col + width]) * inv
        b = _dot(xb, w3_ref[:, col:col + width]) * inv
        hmid = (a * jax.nn.sigmoid(a) * b).astype(BF16)
        part = _dot(hmid, w2_ref[col:col + width, :])
        acc = part if acc is None else acc + part
        col += width
    y = x + 0.5 * acc
    if final_norm:
        y = _rms_norm(y, gf_ref[...])
    if batch_major_in:
        for bi in range(bsz):
            for j in range(SLABS):
                o_ref[j, pl.ds(bi, steps, stride=bsz), :] = (
                    y[bi * steps:(bi + 1) * steps, j * LANES:(j + 1) * LANES])
    else:
        o_ref[...] = y.reshape(bsz, steps, D_MODEL)


def _ffn_call(x, w1, w3, w2, gf, *, batch_major_in, final_norm, bsz, seq):
    steps = min(FFN_STEPS, seq)
    bm_block, tm_block = (bsz, steps, D_MODEL), (SLABS, steps * bsz, LANES)
    in_block, out_block = (bm_block, tm_block) if batch_major_in else (tm_block, bm_block)
    out_shape = (SLABS, seq * bsz, LANES) if batch_major_in else (bsz, seq, D_MODEL)
    return pl.pallas_call(
        functools.partial(_ffn_body, batch_major_in=batch_major_in, final_norm=final_norm,
                          bsz=bsz, steps=steps),
        grid=(seq // steps,),
        in_specs=[pl.BlockSpec(in_block, lambda i: (0, i, 0)),
                  _const_spec((D_MODEL, D_FF)), _const_spec((D_MODEL, D_FF)),
                  _const_spec((D_FF, D_MODEL)), _const_spec((1, D_MODEL))],
        out_specs=pl.BlockSpec(out_block, lambda i: (0, i, 0)),
        out_shape=jax.ShapeDtypeStruct(out_shape, F32),
        compiler_params=pltpu.CompilerParams(
            dimension_semantics=("arbitrary",), vmem_limit_bytes=VMEM_LIMIT),
        name="ffn_final" if final_norm else "ffn",
    )(x, w1, w3, w2, gf)


def _mixer_body(h_ref, w_mix_ref, w_gate_ref, w_alow_ref, a_re_ref, a_im_ref, wb_ref, wc_ref,
                d_ref, glu_w_ref, glu_b_ref, a_up_w_ref, a_up_b_ref, gnorm_ref,
                p_s5_ref, p_gla_ref, w_out_ref, o_ref,
                s5_state, gla_state, bu_buf, loga_buf, bcum_buf, *, bsz, chunk):
    rows = bsz * chunk

    @pl.when(pl.program_id(0) == 0)
    def _():
        s5_state[...] = jnp.zeros_like(s5_state)
        gla_state[...] = jnp.zeros_like(gla_state)

    h = jnp.concatenate([h_ref[j] for j in range(SLABS)], axis=-1)
    u = h.astype(BF16)
    inv = _inv_rms(h)

    def proj(col, width, w_ref=w_mix_ref):
        return _dot(u, w_ref[:, col:col + width]) * inv

    s5_in = proj(C_S5, S5_WIDTH)
    s5_in_bf = s5_in.astype(BF16)
    y_parts = []
    for m in range(S5_BLOCKS):
        buf = bu_buf.at[m]
        buf[...] = _dot(s5_in_bf[:, m * LANES:(m + 1) * LANES], wb_ref[m])
        a_re = jnp.broadcast_to(a_re_ref[m], (bsz, S5_BLOCK_STATE))
        a_im = jnp.broadcast_to(a_im_ref[m], (bsz, S5_BLOCK_STATE))
        s_re = s5_state[m, :, :S5_BLOCK_STATE]
        s_im = s5_state[m, :, S5_BLOCK_STATE:]
        for t in range(chunk):
            r0 = t * bsz
            n_re = a_re * s_re - a_im * s_im + buf[r0:r0 + bsz, :S5_BLOCK_STATE]
            n_im = a_re * s_im + a_im * s_re + buf[r0:r0 + bsz, S5_BLOCK_STATE:]
            buf[r0:r0 + bsz, :S5_BLOCK_STATE] = n_re
            buf[r0:r0 + bsz, S5_BLOCK_STATE:] = n_im
            s_re, s_im = n_re, n_im
        s5_state[m, :, :S5_BLOCK_STATE] = s_re
        s5_state[m, :, S5_BLOCK_STATE:] = s_im
        y_parts.append(_dot(buf[...].astype(BF16), wc_ref[m]))
    y = jnp.concatenate(y_parts, axis=-1) + d_ref[...] * s5_in
    z = jax.nn.gelu(y)
    y_s5 = z * jax.nn.sigmoid(_dot(z.astype(BF16), glu_w_ref[...]) + glu_b_ref[...])

    a_low = proj(0, A_LOW_PAD, w_alow_ref).astype(BF16)
    loga_buf[...] = jax.nn.log_sigmoid(_dot(a_low, a_up_w_ref[...]) + a_up_b_ref[...]) / GLA_TAU

    b_last = jnp.zeros((bsz, GLA_KEY_WIDTH), F32)
    for t in range(chunk):
        b_last = b_last + loga_buf[t * bsz:(t + 1) * bsz, :]
        bcum_buf[t * bsz:(t + 1) * bsz, :] = b_last
    bcum = bcum_buf[...]
    b_last_rows = jnp.broadcast_to(b_last[None], (chunk, bsz, GLA_KEY_WIDTH)).reshape(
        rows, GLA_KEY_WIDTH)
    q = proj(C_Q, GLA_KEY_WIDTH) * (GLA_DK ** -0.5)
    k = proj(C_K, GLA_KEY_WIDTH)
    q_t = q * jnp.exp(bcum)
    k_t = (k * jnp.exp(-bcum)).astype(BF16)
    k_end = k * jnp.exp(b_last_rows - bcum)
    v = proj(C_V, GLA_VAL_WIDTH).astype(BF16)

    row = lax.broadcasted_iota(jnp.int32, (rows, rows), 0)
    col = lax.broadcasted_iota(jnp.int32, (rows, rows), 1)
    causal = ((row % bsz) == (col % bsz)) & (col <= row)
    own_batch = (col // GLA_DK) == (row % bsz)
    own_batch8 = own_batch[:bsz]
    key_lane = lax.broadcasted_iota(jnp.int32, (rows, GLA_KEY_WIDTH), 1) // GLA_DK

    def expand(xh):
        rep = xh + pltpu.roll(xh, 2 * GLA_DK, 1)
        rep = rep + pltpu.roll(rep, GLA_DK, 1)
        return jnp.concatenate([rep, rep], axis=-1)

    o_parts = []
    for hd in range(GLA_HEADS):
        head = key_lane == hd
        q_h = jnp.where(head, q_t, 0.0)
        scores = _dot_nt(q_h.astype(BF16), k_t)
        p = jnp.where(causal, scores, 0.0).astype(BF16)
        v_h = v[:, hd * GLA_DV:(hd + 1) * GLA_DV]
        o_h = _dot(p, v_h)
        q_exp = jnp.where(own_batch, expand(q_h), 0.0).astype(BF16)
        s_prev = gla_state[hd]
        o_h = o_h + _dot_nt(q_exp, s_prev.astype(BF16))
        k_exp = jnp.where(own_batch, expand(jnp.where(head, k_end, 0.0)), 0.0).astype(BF16)
        d_state = _dot_tn(v_h, k_exp)
        bl = expand(jnp.where(head[:bsz], b_last, 0.0))
        decay = jnp.exp(jnp.sum(jnp.where(own_batch8, bl, 0.0), axis=0, keepdims=True))
        gla_state[hd] = decay * s_prev + d_state
        o_parts.append(o_h * lax.rsqrt(jnp.mean(o_h * o_h, axis=-1, keepdims=True) + EPS))
    o = jnp.concatenate(o_parts, axis=-1) * gnorm_ref[...]
    r = proj(C_R, GLA_VAL_WIDTH)
    y_gla = o * (r * jax.nn.sigmoid(r))

    g_s5 = jax.nn.sigmoid(proj(C_GS5, D_MODEL, w_gate_ref))
    g_gla = jax.nn.sigmoid(proj(C_GGLA, D_MODEL, w_gate_ref))
    merged = (g_s5 * _dot(y_s5.astype(BF16), p_s5_ref[...])
              + g_gla * _dot(y_gla.astype(BF16), p_gla_ref[...]))
    out = h + _dot(merged.astype(BF16), w_out_ref[...])
    for j in range(SLABS):
        o_ref[j] = out[:, j * LANES:(j + 1) * LANES]


def _mixer_call(h, w_mix, w_gate, w_alow, a_re, a_im, wb, wc, d_skip, glu_w, glu_b, a_up_w,
                a_up_b, gnorm, p_s5, p_gla, w_out, *, bsz, seq):
    chunk = GLA_CHUNK
    rows = bsz * chunk
    args = (w_mix, w_gate, w_alow, a_re, a_im, wb, wc, d_skip, glu_w, glu_b, a_up_w, a_up_b,
            gnorm, p_s5, p_gla, w_out)
    tile_spec = pl.BlockSpec((SLABS, rows, LANES), lambda i: (0, i, 0))
    return pl.pallas_call(
        functools.partial(_mixer_body, bsz=bsz, chunk=chunk),
        grid=(seq // chunk,),
        in_specs=[tile_spec] + [_const_spec(a.shape) for a in args],
        out_specs=tile_spec,
        out_shape=jax.ShapeDtypeStruct((SLABS, seq * bsz, LANES), F32),
        scratch_shapes=[
            pltpu.VMEM((S5_BLOCKS, bsz, 2 * S5_BLOCK_STATE), F32),
            pltpu.VMEM((GLA_HEADS, GLA_DV, bsz * GLA_DK), F32),
            pltpu.VMEM((S5_BLOCKS, rows, 2 * S5_BLOCK_STATE), F32),
            pltpu.VMEM((rows, GLA_KEY_WIDTH), F32),
            pltpu.VMEM((rows, GLA_KEY_WIDTH), F32),
        ],
        compiler_params=pltpu.CompilerParams(
            dimension_semantics=("arbitrary",), vmem_limit_bytes=VMEM_LIMIT),
        name="mixer",
    )(h, *args)


def _s5_discretize_body(lam_re_ref, lam_im_ref, log_dt_ref, b_re_ref, b_im_ref,
                        ar_ref, ai_ref, bbar_re_ref, bbar_im_ref):
    lam_re = lam_re_ref[...]
    lam_im = lam_im_ref[...]
    dt = jnp.exp(log_dt_ref[...])
    mag = jnp.exp(lam_re * dt)
    ar = mag * jnp.cos(lam_im * dt)
    ai = mag * jnp.sin(lam_im * dt)
    den = lam_re * lam_re + lam_im * lam_im
    nr = ar - 1.0
    fr = ((nr * lam_re + ai * lam_im) / den)[:, None, :]
    fi = ((ai * lam_re - nr * lam_im) / den)[:, None, :]
    ar_ref[...] = ar
    ai_ref[...] = ai
    b_re = b_re_ref[...]
    b_im = b_im_ref[...]
    bbar_re_ref[...] = fr * b_re - fi * b_im
    bbar_im_ref[...] = fr * b_im + fi * b_re


def _s5_discretize(lam_re, lam_im, log_dt, b_re, b_im):
    gp = jax.ShapeDtypeStruct((S5_GROUPS, S5_STATE), F32)
    ghp = jax.ShapeDtypeStruct((S5_GROUPS, S5_GROUP, S5_STATE), F32)
    return pl.pallas_call(
        _s5_discretize_body, out_shape=(gp, gp, ghp, ghp), name="s5_discretize",
    )(lam_re, lam_im, log_dt.reshape(S5_GROUPS, 1),
      jnp.swapaxes(b_re, 1, 2), jnp.swapaxes(b_im, 1, 2))


def _pack_s5(ar, ai, bbar_re, bbar_im, c_re, c_im):
    eye = jnp.eye(S5_BLOCK_GROUPS, dtype=F32)
    nb, ng, ns, nh = S5_BLOCKS, S5_BLOCK_GROUPS, S5_STATE, S5_GROUP
    bb = jnp.stack([bbar_re, bbar_im]).reshape(2, nb, ng, nh, ns)
    wb = jnp.einsum('rmghp,gk->mghrkp', bb, eye).reshape(nb, ng * nh, 2 * ng * ns)
    cc = jnp.stack([c_re, -c_im]).reshape(2, nb, ng, nh, ns)
    wc = jnp.einsum('rmghp,gk->mrgpkh', cc, eye).reshape(nb, 2 * ng * ns, ng * nh)
    a_re = ar.reshape(nb, 1, ng * ns)
    a_im = ai.reshape(nb, 1, ng * ns)
    return a_re, a_im, wb.astype(BF16), wc.astype(BF16)


def kernel(x, ffn1_norm, ffn1_w1, ffn1_w3, ffn1_w2, mix_norm, w_in, s5_lambda_re, s5_lambda_im, s5_log_dt, s5_b_re, s5_b_im, s5_c_re, s5_c_im, s5_d, s5_glu_w, s5_glu_b, gla_a_up_w, gla_a_up_b, gla_out_norm, proj_s5, proj_gla, w_out, ffn2_norm, ffn2_w1, ffn2_w3, ffn2_w2, final_norm):
    bsz, seq, _ = x.shape
    assert bsz == SUBLANES and seq % GLA_CHUNK == 0 and ffn1_norm.shape[0] == 1
    l = 0
    row = lambda a: a.reshape(1, -1).astype(F32)

    def gained(g, w):
        return (g.astype(F32)[:, None] * w).astype(BF16)

    h = _ffn_call(x, gained(ffn1_norm[l], ffn1_w1[l]), gained(ffn1_norm[l], ffn1_w3[l]),
                  ffn1_w2[l].astype(BF16), row(final_norm),
                  batch_major_in=True, final_norm=False, bsz=bsz, seq=seq)

    gate_start = MIX_COLS + GLA_GATE_RANK
    assert IN_SIZES[5] == GLA_GATE_RANK and sum(IN_SIZES[:5]) == MIX_COLS
    w_mix = gained(mix_norm[l], w_in[l][:, :MIX_COLS])
    w_gate = gained(mix_norm[l], w_in[l][:, gate_start:gate_start + GATE_COLS])
    w_alow = jnp.pad(gained(mix_norm[l], w_in[l][:, MIX_COLS:gate_start]),
                     ((0, 0), (0, A_LOW_PAD - GLA_GATE_RANK)))
    a_up_w = jnp.concatenate(
        [gla_a_up_w[l], jnp.zeros((A_LOW_PAD - GLA_GATE_RANK, GLA_KEY_WIDTH), F32)],
        axis=0).astype(BF16)
    ar, ai, bbar_re, bbar_im = _s5_discretize(
        s5_lambda_re[l], s5_lambda_im[l], s5_log_dt[l], s5_b_re[l], s5_b_im[l])
    a_re, a_im, wb, wc = _pack_s5(ar, ai, bbar_re, bbar_im, s5_c_re[l], s5_c_im[l])

    h = _mixer_call(h, w_mix, w_gate, w_alow, a_re, a_im, wb, wc, row(s5_d[l]),
                    s5_glu_w[l].astype(BF16), row(s5_glu_b[l]), a_up_w, row(gla_a_up_b[l]),
                    row(gla_out_norm[l]), proj_s5[l].astype(BF16), proj_gla[l].astype(BF16),
                    w_out[l].astype(BF16), bsz=bsz, seq=seq)

    return _ffn_call(h, gained(ffn2_norm[l], ffn2_w1[l]), gained(ffn2_norm[l], ffn2_w3[l]),
                     ffn2_w2[l].astype(BF16), row(final_norm),
                     batch_major_in=False, final_norm=True, bsz=bsz, seq=seq)
```

```python
import functools
import math

import jax
import jax.numpy as jnp
from jax import lax
from jax.experimental import pallas as pl
from jax.experimental.pallas import tpu as pltpu

D_MODEL = 1024
S5_GROUP = 16
S5_WIDTH = 512
S5_GROUPS = 32
S5_STATE = 64
GLA_HEADS = 4
GLA_DV = 128
GLA_DK = 64
GLA_KEY_WIDTH = 256
GLA_VAL_WIDTH = 512
GLA_GATE_RANK = 16
GLA_TAU = 16.0
GLA_CHUNK = 64
D_FF = 2816
EPS = 1e-6
IN_SIZES = (S5_WIDTH, GLA_KEY_WIDTH, GLA_KEY_WIDTH, GLA_VAL_WIDTH, GLA_VAL_WIDTH,
            GLA_GATE_RANK, D_MODEL, D_MODEL)

LANES = 128
SUBLANES = 8
S5_BLOCKS = 4
S5_BLOCK_GROUPS = S5_GROUPS // S5_BLOCKS
S5_BLOCK_STATE = S5_BLOCK_GROUPS * S5_STATE
A_LOW_PAD = LANES
SLABS = D_MODEL // LANES
FFN_STEPS = 128
FF_CHUNKS = (768, 768, 768, 512)
assert sum(FF_CHUNKS) == D_FF
FF_PREP_COLS = 256
FF_PREP_STEPS = D_FF // FF_PREP_COLS
assert FF_PREP_STEPS * FF_PREP_COLS == D_FF
VMEM_LIMIT = 56 * 1024 * 1024

BF16 = jnp.bfloat16
F32 = jnp.float32

C_S5 = 0
C_Q = C_S5 + S5_WIDTH
C_K = C_Q + GLA_KEY_WIDTH
C_V = C_K + GLA_KEY_WIDTH
C_R = C_V + GLA_VAL_WIDTH
MIX_COLS = C_R + GLA_VAL_WIDTH
C_GS5 = 0
C_GGLA = C_GS5 + D_MODEL
GATE_COLS = C_GGLA + D_MODEL


def _dot(a, b):
    return jnp.dot(a, b, preferred_element_type=F32)


def _dot_nt(a, b):
    return lax.dot_general(a, b, (((1,), (1,)), ((), ())), preferred_element_type=F32)


def _dot_tn(a, b):
    return lax.dot_general(a, b, (((0,), (0,)), ((), ())), preferred_element_type=F32)


def _inv_rms(x):
    return lax.rsqrt(jnp.mean(x * x, axis=-1, keepdims=True) + EPS)


def _rms_norm(x, g):
    return x * _inv_rms(x) * g


def _const_spec(shape):
    nd = len(shape)
    return pl.BlockSpec(shape, lambda *_: (0,) * nd, pipeline_mode=pl.Buffered(1))


def _ffn_body(x_ref, g_ref, w1_ref, w3_ref, w2_ref, gf_ref, o_ref, w1b, w3b, w2b, *,
              batch_major_in, final_norm, bsz, steps):
    step = pl.program_id(0)
    for c in range(FF_PREP_STEPS):
        @pl.when(step == c)
        def _(c=c):
            cols = slice(c * FF_PREP_COLS, (c + 1) * FF_PREP_COLS)
            gain = g_ref[...]
            w1b[:, cols] = (gain * w1_ref[...]).astype(BF16)
            w3b[:, cols] = (gain * w3_ref[...]).astype(BF16)
            w2b[cols, :] = w2_ref[...].astype(BF16)

    @pl.when(step >= FF_PREP_STEPS)
    def _():
        _ffn_tile(x_ref, w1b, w3b, w2b, gf_ref, o_ref, batch_major_in=batch_major_in,
                  final_norm=final_norm, bsz=bsz, steps=steps)


def _ffn_tile(x_ref, w1_ref, w3_ref, w2_ref, gf_ref, o_ref, *,
              batch_major_in, final_norm, bsz, steps):
    rows = bsz * steps
    if batch_major_in:
        x = x_ref[...].reshape(rows, D_MODEL)
    else:
        x = jnp.concatenate(
            [jnp.concatenate([x_ref[j, pl.ds(bi, steps, stride=bsz), :] for j in range(SLABS)],
                             axis=-1) for bi in range(bsz)], axis=0)
    xb = x.astype(BF16)
    inv = _inv_rms(x)
    acc = None
    col = 0
    for width in FF_CHUNKS:
        a = _dot(xb, w1_ref[:, col:col + width]) * inv
        b = _dot(xb, w3_ref[:, col:col + width]) * inv
        hmid = (a * jax.nn.sigmoid(a) * b).astype(BF16)
        part = _dot(hmid, w2_ref[col:col + width, :])
        acc = part if acc is None else acc + part
        col += width
    y = x + 0.5 * acc
    if final_norm:
        y = _rms_norm(y, gf_ref[...])
    if batch_major_in:
        for bi in range(bsz):
            for j in range(SLABS):
                o_ref[j, pl.ds(bi, steps, stride=bsz), :] = (
                    y[bi * steps:(bi + 1) * steps, j * LANES:(j + 1) * LANES])
    else:
        o_ref[...] = y.reshape(bsz, steps, D_MODEL)


def _ffn_call(x, g, w1, w3, w2, gf, *, batch_major_in, final_norm, bsz, seq):
    steps = min(FFN_STEPS, seq)
    bm_block, tm_block = (bsz, steps, D_MODEL), (SLABS, steps * bsz, LANES)
    in_block, out_block = (bm_block, tm_block) if batch_major_in else (tm_block, bm_block)
    out_shape = (SLABS, seq * bsz, LANES) if batch_major_in else (bsz, seq, D_MODEL)
    tile = lambda i: (0, jnp.maximum(i - FF_PREP_STEPS, 0), 0)
    prep = lambda i: jnp.minimum(i, FF_PREP_STEPS - 1)
    return pl.pallas_call(
        functools.partial(_ffn_body, batch_major_in=batch_major_in, final_norm=final_norm,
                          bsz=bsz, steps=steps),
        grid=(FF_PREP_STEPS + seq // steps,),
        in_specs=[pl.BlockSpec(in_block, tile),
                  _const_spec((D_MODEL, 1)),
                  pl.BlockSpec((D_MODEL, FF_PREP_COLS), lambda i: (0, prep(i))),
                  pl.BlockSpec((D_MODEL, FF_PREP_COLS), lambda i: (0, prep(i))),
                  pl.BlockSpec((FF_PREP_COLS, D_MODEL), lambda i: (prep(i), 0)),
                  _const_spec((1, D_MODEL))],
        out_specs=pl.BlockSpec(out_block, tile),
        out_shape=jax.ShapeDtypeStruct(out_shape, F32),
        scratch_shapes=[pltpu.VMEM((D_MODEL, D_FF), BF16), pltpu.VMEM((D_MODEL, D_FF), BF16),
                        pltpu.VMEM((D_FF, D_MODEL), BF16)],
        compiler_params=pltpu.CompilerParams(
            dimension_semantics=("arbitrary",), vmem_limit_bytes=VMEM_LIMIT),
        name="ffn_final" if final_norm else "ffn",
    )(x, g, w1, w3, w2, gf)


def _mixer_body(h_ref, w_mix_ref, w_gate_ref, w_alow_ref, a_re_ref, a_im_ref, wb_ref, wc_ref,
                d_ref, glu_w_ref, glu_b_ref, a_up_w_ref, a_up_b_ref, gnorm_ref,
                p_s5_ref, p_gla_ref, w_out_ref, o_ref,
                s5_state, gla_state, bu_buf, loga_buf, bcum_buf, *, bsz, chunk):
    rows = bsz * chunk

    @pl.when(pl.program_id(0) == 0)
    def _():
        s5_state[...] = jnp.zeros_like(s5_state)
        gla_state[...] = jnp.zeros_like(gla_state)

    h = jnp.concatenate([h_ref[j] for j in range(SLABS)], axis=-1)
    u = h.astype(BF16)
    inv = _inv_rms(h)

    def proj(col, width, w_ref=w_mix_ref):
        return _dot(u, w_ref[:, col:col + width]) * inv

    s5_in = proj(C_S5, S5_WIDTH)
    s5_in_bf = s5_in.astype(BF16)
    y_parts = []
    for m in range(S5_BLOCKS):
        buf = bu_buf.at[m]
        buf[...] = _dot(s5_in_bf[:, m * LANES:(m + 1) * LANES], wb_ref[m])
        a_re = jnp.broadcast_to(a_re_ref[m], (bsz, S5_BLOCK_STATE))
        a_im = jnp.broadcast_to(a_im_ref[m], (bsz, S5_BLOCK_STATE))
        s_re = s5_state[m, :, :S5_BLOCK_STATE]
        s_im = s5_state[m, :, S5_BLOCK_STATE:]
        for t in range(chunk):
            r0 = t * bsz
            n_re = a_re * s_re - a_im * s_im + buf[r0:r0 + bsz, :S5_BLOCK_STATE]
            n_im = a_re * s_im + a_im * s_re + buf[r0:r0 + bsz, S5_BLOCK_STATE:]
            buf[r0:r0 + bsz, :S5_BLOCK_STATE] = n_re
            buf[r0:r0 + bsz, S5_BLOCK_STATE:] = n_im
            s_re, s_im = n_re, n_im
        s5_state[m, :, :S5_BLOCK_STATE] = s_re
        s5_state[m, :, S5_BLOCK_STATE:] = s_im
        y_parts.append(_dot(buf[...].astype(BF16), wc_ref[m]))
    y = jnp.concatenate(y_parts, axis=-1) + d_ref[...] * s5_in
    z = jax.nn.gelu(y)
    y_s5 = z * jax.nn.sigmoid(_dot(z.astype(BF16), glu_w_ref[...]) + glu_b_ref[...])

    a_low = proj(0, A_LOW_PAD, w_alow_ref).astype(BF16)
    loga_buf[...] = jax.nn.log_sigmoid(_dot(a_low, a_up_w_ref[...]) + a_up_b_ref[...]) / GLA_TAU

    b_last = jnp.zeros((bsz, GLA_KEY_WIDTH), F32)
    for t in range(chunk):
        b_last = b_last + loga_buf[t * bsz:(t + 1) * bsz, :]
        bcum_buf[t * bsz:(t + 1) * bsz, :] = b_last
    bcum = bcum_buf[...]
    b_last_rows = jnp.broadcast_to(b_last[None], (chunk, bsz, GLA_KEY_WIDTH)).reshape(
        rows, GLA_KEY_WIDTH)
    q = proj(C_Q, GLA_KEY_WIDTH) * (GLA_DK ** -0.5)
    k = proj(C_K, GLA_KEY_WIDTH)
    q_t = q * jnp.exp(bcum)
    k_t = (k * jnp.exp(-bcum)).astype(BF16)
    k_end = k * jnp.exp(b_last_rows - bcum)
    v = proj(C_V, GLA_VAL_WIDTH).astype(BF16)

    row = lax.broadcasted_iota(jnp.int32, (rows, rows), 0)
    col = lax.broadcasted_iota(jnp.int32, (rows, rows), 1)
    causal = ((row % bsz) == (col % bsz)) & (col <= row)
    own_batch = (col // GLA_DK) == (row % bsz)
    own_batch8 = own_batch[:bsz]
    key_lane = lax.broadcasted_iota(jnp.int32, (rows, GLA_KEY_WIDTH), 1) // GLA_DK

    def expand(xh):
        rep = xh + pltpu.roll(xh, 2 * GLA_DK, 1)
        rep = rep + pltpu.roll(rep, GLA_DK, 1)
        return jnp.concatenate([rep, rep], axis=-1)

    o_parts = []
    for hd in range(GLA_HEADS):
        head = key_lane == hd
        q_h = jnp.where(head, q_t, 0.0)
        scores = _dot_nt(q_h.astype(BF16), k_t)
        p = jnp.where(causal, scores, 0.0).astype(BF16)
        v_h = v[:, hd * GLA_DV:(hd + 1) * GLA_DV]
        o_h = _dot(p, v_h)
        q_exp = jnp.where(own_batch, expand(q_h), 0.0).astype(BF16)
        s_prev = gla_state[hd]
        o_h = o_h + _dot_nt(q_exp, s_prev.astype(BF16))
        k_exp = jnp.where(own_batch, expand(jnp.where(head, k_end, 0.0)), 0.0).astype(BF16)
        d_state = _dot_tn(v_h, k_exp)
        bl = expand(jnp.where(head[:bsz], b_last, 0.0))
        decay = jnp.exp(jnp.sum(jnp.where(own_batch8, bl, 0.0), axis=0, keepdims=True))
        gla_state[hd] = decay * s_prev + d_state
        o_parts.append(o_h * lax.rsqrt(jnp.mean(o_h * o_h, axis=-1, keepdims=True) + EPS))
    o = jnp.concatenate(o_parts, axis=-1) * gnorm_ref[...]
    r = proj(C_R, GLA_VAL_WIDTH)
    y_gla = o * (r * jax.nn.sigmoid(r))

    g_s5 = jax.nn.sigmoid(proj(C_GS5, D_MODEL, w_gate_ref))
    g_gla = jax.nn.sigmoid(proj(C_GGLA, D_MODEL, w_gate_ref))
    merged = (g_s5 * _dot(y_s5.astype(BF16), p_s5_ref[...])
              + g_gla * _dot(y_gla.astype(BF16), p_gla_ref[...]))
    out = h + _dot(merged.astype(BF16), w_out_ref[...])
    for j in range(SLABS):
        o_ref[j] = out[:, j * LANES:(j + 1) * LANES]


def _mixer_call(h, w_mix, w_gate, w_alow, a_re, a_im, wb, wc, d_skip, glu_w, glu_b, a_up_w,
                a_up_b, gnorm, p_s5, p_gla, w_out, *, bsz, seq):
    chunk = GLA_CHUNK
    rows = bsz * chunk
    args = (w_mix, w_gate, w_alow, a_re, a_im, wb, wc, d_skip, glu_w, glu_b, a_up_w, a_up_b,
            gnorm, p_s5, p_gla, w_out)
    tile_spec = pl.BlockSpec((SLABS, rows, LANES), lambda i: (0, i, 0))
    return pl.pallas_call(
        functools.partial(_mixer_body, bsz=bsz, chunk=chunk),
        grid=(seq // chunk,),
        in_specs=[tile_spec] + [_const_spec(a.shape) for a in args],
        out_specs=tile_spec,
        out_shape=jax.ShapeDtypeStruct((SLABS, seq * bsz, LANES), F32),
        scratch_shapes=[
            pltpu.VMEM((S5_BLOCKS, bsz, 2 * S5_BLOCK_STATE), F32),
            pltpu.VMEM((GLA_HEADS, GLA_DV, bsz * GLA_DK), F32),
            pltpu.VMEM((S5_BLOCKS, rows, 2 * S5_BLOCK_STATE), F32),
            pltpu.VMEM((rows, GLA_KEY_WIDTH), F32),
            pltpu.VMEM((rows, GLA_KEY_WIDTH), F32),
        ],
        compiler_params=pltpu.CompilerParams(
            dimension_semantics=("arbitrary",), vmem_limit_bytes=VMEM_LIMIT),
        name="mixer",
    )(h, *args)


def _s5_discretize_body(lam_re_ref, lam_im_ref, log_dt_ref, b_re_ref, b_im_ref,
                        ar_ref, ai_ref, bbar_re_ref, bbar_im_ref):
    lam_re = lam_re_ref[...]
    lam_im = lam_im_ref[...]
    dt = jnp.exp(log_dt_ref[...])
    mag = jnp.exp(lam_re * dt)
    ar = mag * jnp.cos(lam_im * dt)
    ai = mag * jnp.sin(lam_im * dt)
    den = lam_re * lam_re + lam_im * lam_im
    nr = ar - 1.0
    fr = ((nr * lam_re + ai * lam_im) / den)[:, None, :]
    fi = ((ai * lam_re - nr * lam_im) / den)[:, None, :]
    ar_ref[...] = ar
    ai_ref[...] = ai
    b_re = b_re_ref[...]
    b_im = b_im_ref[...]
    bbar_re_ref[...] = fr * b_re - fi * b_im
    bbar_im_ref[...] = fr * b_im + fi * b_re


def _s5_discretize(lam_re, lam_im, log_dt, b_re, b_im):
    gp = jax.ShapeDtypeStruct((S5_GROUPS, S5_STATE), F32)
    ghp = jax.ShapeDtypeStruct((S5_GROUPS, S5_GROUP, S5_STATE), F32)
    return pl.pallas_call(
        _s5_discretize_body, out_shape=(gp, gp, ghp, ghp), name="s5_discretize",
    )(lam_re, lam_im, log_dt.reshape(S5_GROUPS, 1),
      jnp.swapaxes(b_re, 1, 2), jnp.swapaxes(b_im, 1, 2))


def _pack_s5(ar, ai, bbar_re, bbar_im, c_re, c_im):
    eye = jnp.eye(S5_BLOCK_GROUPS, dtype=F32)
    nb, ng, ns, nh = S5_BLOCKS, S5_BLOCK_GROUPS, S5_STATE, S5_GROUP
    bb = jnp.stack([bbar_re, bbar_im]).reshape(2, nb, ng, nh, ns)
    wb = jnp.einsum('rmghp,gk->mghrkp', bb, eye).reshape(nb, ng * nh, 2 * ng * ns)
    cc = jnp.stack([c_re, -c_im]).reshape(2, nb, ng, nh, ns)
    wc = jnp.einsum('rmghp,gk->mrgpkh', cc, eye).reshape(nb, 2 * ng * ns, ng * nh)
    a_re = ar.reshape(nb, 1, ng * ns)
    a_im = ai.reshape(nb, 1, ng * ns)
    return a_re, a_im, wb.astype(BF16), wc.astype(BF16)


def kernel(x, ffn1_norm, ffn1_w1, ffn1_w3, ffn1_w2, mix_norm, w_in, s5_lambda_re, s5_lambda_im, s5_log_dt, s5_b_re, s5_b_im, s5_c_re, s5_c_im, s5_d, s5_glu_w, s5_glu_b, gla_a_up_w, gla_a_up_b, gla_out_norm, proj_s5, proj_gla, w_out, ffn2_norm, ffn2_w1, ffn2_w3, ffn2_w2, final_norm):
    bsz, seq, _ = x.shape
    assert bsz == SUBLANES and seq % GLA_CHUNK == 0 and ffn1_norm.shape[0] == 1
    l = 0
    row = lambda a: a.reshape(1, -1).astype(F32)

    def gained(g, w):
        return (g.astype(F32)[:, None] * w).astype(BF16)

    col = lambda a: a.reshape(-1, 1).astype(F32)
    h = _ffn_call(x, col(ffn1_norm[l]), ffn1_w1[l], ffn1_w3[l], ffn1_w2[l], row(final_norm),
                  batch_major_in=True, final_norm=False, bsz=bsz, seq=seq)

    gate_start = MIX_COLS + GLA_GATE_RANK
    assert IN_SIZES[5] == GLA_GATE_RANK and sum(IN_SIZES[:5]) == MIX_COLS
    w_in_g = gained(mix_norm[l], w_in[l])
    w_mix = w_in_g[:, :MIX_COLS]
    w_gate = w_in_g[:, gate_start:gate_start + GATE_COLS]
    w_alow = jnp.pad(w_in_g[:, MIX_COLS:gate_start], ((0, 0), (0, A_LOW_PAD - GLA_GATE_RANK)))
    a_up_w = jnp.concatenate(
        [gla_a_up_w[l], jnp.zeros((A_LOW_PAD - GLA_GATE_RANK, GLA_KEY_WIDTH), F32)],
        axis=0).astype(BF16)
    ar, ai, bbar_re, bbar_im = _s5_discretize(
        s5_lambda_re[l], s5_lambda_im[l], s5_log_dt[l], s5_b_re[l], s5_b_im[l])
    a_re, a_im, wb, wc = _pack_s5(ar, ai, bbar_re, bbar_im, s5_c_re[l], s5_c_im[l])

    h = _mixer_call(h, w_mix, w_gate, w_alow, a_re, a_im, wb, wc, row(s5_d[l]),
                    s5_glu_w[l].astype(BF16), row(s5_glu_b[l]), a_up_w, row(gla_a_up_b[l]),
                    row(gla_out_norm[l]), proj_s5[l].astype(BF16), proj_gla[l].astype(BF16),
                    w_out[l].astype(BF16), bsz=bsz, seq=seq)

    return _ffn_call(h, col(ffn2_norm[l]), ffn2_w1[l], ffn2_w3[l], ffn2_w2[l], row(final_norm),
                     batch_major_in=False, final_norm=True, bsz=bsz, seq=seq)
```

```python
import functools

import jax
import jax.numpy as jnp
from jax import lax
from jax.experimental import pallas as pl
from jax.experimental.pallas import tpu as pltpu

D_MODEL = 1024
S5_GROUP = 16
S5_WIDTH = 512
S5_GROUPS = 32
S5_STATE = 64
GLA_HEADS = 4
GLA_DV = 128
GLA_DK = 64
GLA_KEY_WIDTH = 256
GLA_VAL_WIDTH = 512
GLA_GATE_RANK = 16
GLA_TAU = 16.0
GLA_CHUNK = 64
D_FF = 2816
EPS = 1e-6
IN_SIZES = (S5_WIDTH, GLA_KEY_WIDTH, GLA_KEY_WIDTH, GLA_VAL_WIDTH, GLA_VAL_WIDTH,
            GLA_GATE_RANK, D_MODEL, D_MODEL)

LANES = 128
SUBLANES = 8
S5_BLOCKS = 4
S5_BLOCK_GROUPS = S5_GROUPS // S5_BLOCKS
S5_BLOCK_STATE = S5_BLOCK_GROUPS * S5_STATE
A_LOW_PAD = LANES
SLABS = D_MODEL // LANES
FFN_STEPS = 128
MIX_STEPS = 128
FF_CHUNKS = (768, 768, 768, 512)
assert sum(FF_CHUNKS) == D_FF
FF_PREP_COLS = 256
FF_PREP_STEPS = D_FF // FF_PREP_COLS
assert FF_PREP_STEPS * FF_PREP_COLS == D_FF
VMEM_LIMIT = 56 * 1024 * 1024

BF16 = jnp.bfloat16
F32 = jnp.float32

C_S5 = 0
C_Q = C_S5 + S5_WIDTH
C_K = C_Q + GLA_KEY_WIDTH
C_V = C_K + GLA_KEY_WIDTH
C_R = C_V + GLA_VAL_WIDTH
MIX_COLS = C_R + GLA_VAL_WIDTH
C_GS5 = 0
C_GGLA = C_GS5 + D_MODEL
GATE_COLS = C_GGLA + D_MODEL


def _dot(a, b):
    return jnp.dot(a, b, preferred_element_type=F32)


def _dot_nt(a, b):
    return lax.dot_general(a, b, (((1,), (1,)), ((), ())), preferred_element_type=F32)


def _dot_tn(a, b):
    return lax.dot_general(a, b, (((0,), (0,)), ((), ())), preferred_element_type=F32)


def _inv_rms(x):
    return lax.rsqrt(jnp.mean(x * x, axis=-1, keepdims=True) + EPS)


def _rms_norm(x, g):
    return x * _inv_rms(x) * g


def _const_spec(shape):
    nd = len(shape)
    return pl.BlockSpec(shape, lambda *_: (0,) * nd, pipeline_mode=pl.Buffered(1))


def _ffn_body(x_ref, g_ref, w1_ref, w3_ref, w2_ref, gf_ref, o_ref, w1b, w3b, w2b, *,
              batch_major_in, final_norm, bsz, steps):
    step = pl.program_id(0)
    for c in range(FF_PREP_STEPS):
        @pl.when(step == c)
        def _(c=c):
            cols = slice(c * FF_PREP_COLS, (c + 1) * FF_PREP_COLS)
            gain = g_ref[...]
            w1b[:, cols] = (gain * w1_ref[...]).astype(BF16)
            w3b[:, cols] = (gain * w3_ref[...]).astype(BF16)
            w2b[cols, :] = w2_ref[...].astype(BF16)

    @pl.when(step >= FF_PREP_STEPS)
    def _():
        _ffn_tile(x_ref, w1b, w3b, w2b, gf_ref, o_ref, batch_major_in=batch_major_in,
                  final_norm=final_norm, bsz=bsz, steps=steps)


def _ffn_tile(x_ref, w1_ref, w3_ref, w2_ref, gf_ref, o_ref, *,
              batch_major_in, final_norm, bsz, steps):
    rows = bsz * steps
    if batch_major_in:
        x = x_ref[...].reshape(rows, D_MODEL)
    else:
        x = jnp.concatenate(
            [jnp.concatenate([x_ref[j, pl.ds(bi, steps, stride=bsz), :] for j in range(SLABS)],
                             axis=-1) for bi in range(bsz)], axis=0)
    xb = x.astype(BF16)
    inv = _inv_rms(x)
    acc = None
    col = 0
    for width in FF_CHUNKS:
        a = _dot(xb, w1_ref[:, col:col + width]) * inv
        b = _dot(xb, w3_ref[:, col:col + width]) * inv
        hmid = (a * jax.nn.sigmoid(a) * b).astype(BF16)
        part = _dot(hmid, w2_ref[col:col + width, :])
        acc = part if acc is None else acc + part
        col += width
    y = x + 0.5 * acc
    if final_norm:
        y = _rms_norm(y, gf_ref[...])
    if batch_major_in:
        for bi in range(bsz):
            for j in range(SLABS):
                o_ref[j, pl.ds(bi, steps, stride=bsz), :] = (
                    y[bi * steps:(bi + 1) * steps, j * LANES:(j + 1) * LANES])
    else:
        o_ref[...] = y.reshape(bsz, steps, D_MODEL)


def _ffn_call(x, g, w1, w3, w2, gf, *, batch_major_in, final_norm, bsz, seq):
    steps = min(FFN_STEPS, seq)
    bm_block, tm_block = (bsz, steps, D_MODEL), (SLABS, steps * bsz, LANES)
    in_block, out_block = (bm_block, tm_block) if batch_major_in else (tm_block, bm_block)
    out_shape = (SLABS, seq * bsz, LANES) if batch_major_in else (bsz, seq, D_MODEL)
    tile = lambda i: (0, jnp.maximum(i - FF_PREP_STEPS, 0), 0)
    prep = lambda i: jnp.minimum(i, FF_PREP_STEPS - 1)
    return pl.pallas_call(
        functools.partial(_ffn_body, batch_major_in=batch_major_in, final_norm=final_norm,
                          bsz=bsz, steps=steps),
        grid=(FF_PREP_STEPS + seq // steps,),
        in_specs=[pl.BlockSpec(in_block, tile),
                  _const_spec((D_MODEL, 1)),
                  pl.BlockSpec((D_MODEL, FF_PREP_COLS), lambda i: (0, prep(i))),
                  pl.BlockSpec((D_MODEL, FF_PREP_COLS), lambda i: (0, prep(i))),
                  pl.BlockSpec((FF_PREP_COLS, D_MODEL), lambda i: (prep(i), 0)),
                  _const_spec((1, D_MODEL))],
        out_specs=pl.BlockSpec(out_block, tile),
        out_shape=jax.ShapeDtypeStruct(out_shape, F32),
        scratch_shapes=[pltpu.VMEM((D_MODEL, D_FF), BF16), pltpu.VMEM((D_MODEL, D_FF), BF16),
                        pltpu.VMEM((D_FF, D_MODEL), BF16)],
        compiler_params=pltpu.CompilerParams(
            dimension_semantics=("arbitrary",), vmem_limit_bytes=VMEM_LIMIT),
        name="ffn_final" if final_norm else "ffn",
    )(x, g, w1, w3, w2, gf)


def _mixer_body(h_ref, w_mix_ref, w_gate_ref, w_alow_ref, a_re_ref, a_im_ref, wb_ref, wc_ref,
                d_ref, glu_w_ref, glu_b_ref, a_up_w_ref, a_up_b_ref, gnorm_ref,
                p_s5_ref, p_gla_ref, w_out_ref, o_ref,
                s5_state, gla_state, bu_buf, loga_buf, bcum_buf, *, bsz, steps, chunk):
    rows = bsz * steps
    crows = bsz * chunk

    @pl.when(pl.program_id(0) == 0)
    def _():
        s5_state[...] = jnp.zeros_like(s5_state)
        gla_state[...] = jnp.zeros_like(gla_state)

    h = jnp.concatenate([h_ref[j] for j in range(SLABS)], axis=-1)
    u = h.astype(BF16)
    inv = _inv_rms(h)

    def proj(col, width, w_ref=w_mix_ref):
        return _dot_nt(u, w_ref[col:col + width, :]) * inv

    s5_in = proj(C_S5, S5_WIDTH)
    s5_in_bf = s5_in.astype(BF16)
    y_parts = []
    for m in range(S5_BLOCKS):
        buf = bu_buf.at[m]
        buf[...] = _dot(s5_in_bf[:, m * LANES:(m + 1) * LANES], wb_ref[m])
        a_re = jnp.broadcast_to(a_re_ref[m], (bsz, S5_BLOCK_STATE))
        a_im = jnp.broadcast_to(a_im_ref[m], (bsz, S5_BLOCK_STATE))
        s_re = s5_state[m, :, :S5_BLOCK_STATE]
        s_im = s5_state[m, :, S5_BLOCK_STATE:]
        for t in range(steps):
            r0 = t * bsz
            n_re = a_re * s_re - a_im * s_im + buf[r0:r0 + bsz, :S5_BLOCK_STATE]
            n_im = a_re * s_im + a_im * s_re + buf[r0:r0 + bsz, S5_BLOCK_STATE:]
            buf[r0:r0 + bsz, :S5_BLOCK_STATE] = n_re
            buf[r0:r0 + bsz, S5_BLOCK_STATE:] = n_im
            s_re, s_im = n_re, n_im
        s5_state[m, :, :S5_BLOCK_STATE] = s_re
        s5_state[m, :, S5_BLOCK_STATE:] = s_im
        y_parts.append(_dot(buf[...].astype(BF16), wc_ref[m]))
    y = jnp.concatenate(y_parts, axis=-1) + d_ref[...] * s5_in
    z = jax.nn.gelu(y)
    y_s5 = z * jax.nn.sigmoid(_dot(z.astype(BF16), glu_w_ref[...]) + glu_b_ref[...])

    a_low = proj(0, A_LOW_PAD, w_alow_ref).astype(BF16)
    loga_buf[...] = jax.nn.log_sigmoid(_dot(a_low, a_up_w_ref[...]) + a_up_b_ref[...]) / GLA_TAU

    q = proj(C_Q, GLA_KEY_WIDTH) * (GLA_DK ** -0.5)
    k = proj(C_K, GLA_KEY_WIDTH)
    v = proj(C_V, GLA_VAL_WIDTH).astype(BF16)

    row = lax.broadcasted_iota(jnp.int32, (crows, crows), 0)
    col = lax.broadcasted_iota(jnp.int32, (crows, crows), 1)
    causal = ((row % bsz) == (col % bsz)) & (col <= row)
    own_batch = (col // GLA_DK) == (row % bsz)
    own_batch8 = own_batch[:bsz]
    key_lane = lax.broadcasted_iota(jnp.int32, (crows, GLA_KEY_WIDTH), 1) // GLA_DK

    def expand(xh):
        rep = xh + pltpu.roll(xh, 2 * GLA_DK, 1)
        rep = rep + pltpu.roll(rep, GLA_DK, 1)
        return jnp.concatenate([rep, rep], axis=-1)

    o_chunks = []
    for c in range(steps // chunk):
        c0 = c * crows
        b_last = jnp.zeros((bsz, GLA_KEY_WIDTH), F32)
        for t in range(chunk):
            r0 = c0 + t * bsz
            b_last = b_last + loga_buf[r0:r0 + bsz, :]
            bcum_buf[r0:r0 + bsz, :] = b_last
        bcum = bcum_buf[c0:c0 + crows, :]
        b_last_rows = jnp.broadcast_to(b_last[None], (chunk, bsz, GLA_KEY_WIDTH)).reshape(
            crows, GLA_KEY_WIDTH)
        k_c = k[c0:c0 + crows]
        q_t = q[c0:c0 + crows] * jnp.exp(bcum)
        k_t = (k_c * jnp.exp(-bcum)).astype(BF16)
        k_end = k_c * jnp.exp(b_last_rows - bcum)
        o_parts = []
        for hd in range(GLA_HEADS):
            head = key_lane == hd
            q_h = jnp.where(head, q_t, 0.0)
            scores = _dot_nt(q_h.astype(BF16), k_t)
            p = jnp.where(causal, scores, 0.0).astype(BF16)
            v_h = v[c0:c0 + crows, hd * GLA_DV:(hd + 1) * GLA_DV]
            o_h = _dot(p, v_h)
            q_exp = jnp.where(own_batch, expand(q_h), 0.0).astype(BF16)
            s_prev = gla_state[hd]
            o_h = o_h + _dot_nt(q_exp, s_prev.astype(BF16))
            k_exp = jnp.where(own_batch, expand(jnp.where(head, k_end, 0.0)), 0.0).astype(BF16)
            d_state = _dot_tn(v_h, k_exp)
            bl = expand(jnp.where(head[:bsz], b_last, 0.0))
            decay = jnp.exp(jnp.sum(jnp.where(own_batch8, bl, 0.0), axis=0, keepdims=True))
            gla_state[hd] = decay * s_prev + d_state
            o_parts.append(o_h * lax.rsqrt(jnp.mean(o_h * o_h, axis=-1, keepdims=True) + EPS))
        o_chunks.append(jnp.concatenate(o_parts, axis=-1))
    o = jnp.concatenate(o_chunks, axis=0) * gnorm_ref[...]
    r = proj(C_R, GLA_VAL_WIDTH)
    y_gla = o * (r * jax.nn.sigmoid(r))

    g_s5 = jax.nn.sigmoid(proj(C_GS5, D_MODEL, w_gate_ref))
    g_gla = jax.nn.sigmoid(proj(C_GGLA, D_MODEL, w_gate_ref))
    merged = (g_s5 * _dot(y_s5.astype(BF16), p_s5_ref[...])
              + g_gla * _dot(y_gla.astype(BF16), p_gla_ref[...]))
    out = h + _dot(merged.astype(BF16), w_out_ref[...])
    for j in range(SLABS):
        o_ref[j] = out[:, j * LANES:(j + 1) * LANES]


def _mixer_call(h, w_mix, w_gate, w_alow, a_re, a_im, wb, wc, d_skip, glu_w, glu_b, a_up_w,
                a_up_b, gnorm, p_s5, p_gla, w_out, *, bsz, seq):
    steps = min(MIX_STEPS, seq)
    rows = bsz * steps
    args = (w_mix, w_gate, w_alow, a_re, a_im, wb, wc, d_skip, glu_w, glu_b, a_up_w, a_up_b,
            gnorm, p_s5, p_gla, w_out)
    tile_spec = pl.BlockSpec((SLABS, rows, LANES), lambda i: (0, i, 0))
    return pl.pallas_call(
        functools.partial(_mixer_body, bsz=bsz, steps=steps, chunk=GLA_CHUNK),
        grid=(seq // steps,),
        in_specs=[tile_spec] + [_const_spec(a.shape) for a in args],
        out_specs=tile_spec,
        out_shape=jax.ShapeDtypeStruct((SLABS, seq * bsz, LANES), F32),
        scratch_shapes=[
            pltpu.VMEM((S5_BLOCKS, bsz, 2 * S5_BLOCK_STATE), F32),
            pltpu.VMEM((GLA_HEADS, GLA_DV, bsz * GLA_DK), F32),
            pltpu.VMEM((S5_BLOCKS, rows, 2 * S5_BLOCK_STATE), F32),
            pltpu.VMEM((rows, GLA_KEY_WIDTH), F32),
            pltpu.VMEM((rows, GLA_KEY_WIDTH), F32),
        ],
        compiler_params=pltpu.CompilerParams(
            dimension_semantics=("arbitrary",), vmem_limit_bytes=VMEM_LIMIT),
        name="mixer",
    )(h, *args)


def _s5_discretize_body(lam_re_ref, lam_im_ref, log_dt_ref, b_re_ref, b_im_ref,
                        ar_ref, ai_ref, bbar_re_ref, bbar_im_ref):
    lam_re = lam_re_ref[...]
    lam_im = lam_im_ref[...]
    dt = jnp.exp(log_dt_ref[...])
    mag = jnp.exp(lam_re * dt)
    ar = mag * jnp.cos(lam_im * dt)
    ai = mag * jnp.sin(lam_im * dt)
    den = lam_re * lam_re + lam_im * lam_im
    nr = ar - 1.0
    fr = ((nr * lam_re + ai * lam_im) / den)[:, None, :]
    fi = ((ai * lam_re - nr * lam_im) / den)[:, None, :]
    ar_ref[...] = ar
    ai_ref[...] = ai
    b_re = b_re_ref[...]
    b_im = b_im_ref[...]
    bbar_re_ref[...] = fr * b_re - fi * b_im
    bbar_im_ref[...] = fr * b_im + fi * b_re


def _s5_discretize(lam_re, lam_im, log_dt, b_re, b_im):
    gp = jax.ShapeDtypeStruct((S5_GROUPS, S5_STATE), F32)
    ghp = jax.ShapeDtypeStruct((S5_GROUPS, S5_GROUP, S5_STATE), F32)
    return pl.pallas_call(
        _s5_discretize_body, out_shape=(gp, gp, ghp, ghp), name="s5_discretize",
    )(lam_re, lam_im, log_dt.reshape(S5_GROUPS, 1),
      jnp.swapaxes(b_re, 1, 2), jnp.swapaxes(b_im, 1, 2))


def _pack_s5(ar, ai, bbar_re, bbar_im, c_re, c_im):
    eye = jnp.eye(S5_BLOCK_GROUPS, dtype=F32)
    nb, ng, ns, nh = S5_BLOCKS, S5_BLOCK_GROUPS, S5_STATE, S5_GROUP
    bb = jnp.stack([bbar_re, bbar_im]).reshape(2, nb, ng, nh, ns)
    wb = jnp.einsum('rmghp,gk->mghrkp', bb, eye).reshape(nb, ng * nh, 2 * ng * ns)
    cc = jnp.stack([c_re, -c_im]).reshape(2, nb, ng, nh, ns)
    wc = jnp.einsum('rmghp,gk->mrgpkh', cc, eye).reshape(nb, 2 * ng * ns, ng * nh)
    a_re = ar.reshape(nb, 1, ng * ns)
    a_im = ai.reshape(nb, 1, ng * ns)
    return a_re, a_im, wb.astype(BF16), wc.astype(BF16)


def kernel(x, ffn1_norm, ffn1_w1, ffn1_w3, ffn1_w2, mix_norm, w_in, s5_lambda_re, s5_lambda_im, s5_log_dt, s5_b_re, s5_b_im, s5_c_re, s5_c_im, s5_d, s5_glu_w, s5_glu_b, gla_a_up_w, gla_a_up_b, gla_out_norm, proj_s5, proj_gla, w_out, ffn2_norm, ffn2_w1, ffn2_w3, ffn2_w2, final_norm):
    bsz, seq, _ = x.shape
    assert bsz == SUBLANES and seq % GLA_CHUNK == 0 and ffn1_norm.shape[0] == 1
    assert seq % min(FFN_STEPS, seq) == 0 and seq % min(MIX_STEPS, seq) == 0
    assert MIX_STEPS % GLA_CHUNK == 0
    l = 0
    row = lambda a: a.reshape(1, -1).astype(F32)

    col = lambda a: a.reshape(-1, 1).astype(F32)
    h = _ffn_call(x, col(ffn1_norm[l]), ffn1_w1[l], ffn1_w3[l], ffn1_w2[l], row(final_norm),
                  batch_major_in=True, final_norm=False, bsz=bsz, seq=seq)

    assert IN_SIZES[5] == GLA_GATE_RANK and sum(IN_SIZES[:5]) == MIX_COLS
    w_in_t = (jnp.swapaxes(w_in, 1, 2)[l] * mix_norm[l].astype(F32)[None, :]).astype(BF16)
    gate_start = MIX_COLS + GLA_GATE_RANK
    w_mix = w_in_t[:MIX_COLS]
    w_gate = w_in_t[gate_start:gate_start + GATE_COLS]
    w_alow = jnp.pad(w_in_t[MIX_COLS:gate_start], ((0, A_LOW_PAD - GLA_GATE_RANK), (0, 0)))
    a_up_w = jnp.concatenate(
        [gla_a_up_w[l], jnp.zeros((A_LOW_PAD - GLA_GATE_RANK, GLA_KEY_WIDTH), F32)],
        axis=0).astype(BF16)
    ar, ai, bbar_re, bbar_im = _s5_discretize(
        s5_lambda_re[l], s5_lambda_im[l], s5_log_dt[l], s5_b_re[l], s5_b_im[l])
    a_re, a_im, wb, wc = _pack_s5(ar, ai, bbar_re, bbar_im, s5_c_re[l], s5_c_im[l])

    h = _mixer_call(h, w_mix, w_gate, w_alow, a_re, a_im, wb, wc, row(s5_d[l]),
                    s5_glu_w[l].astype(BF16), row(s5_glu_b[l]), a_up_w, row(gla_a_up_b[l]),
                    row(gla_out_norm[l]), proj_s5[l].astype(BF16), proj_gla[l].astype(BF16),
                    w_out[l].astype(BF16), bsz=bsz, seq=seq)

    return _ffn_call(h, col(ffn2_norm[l]), ffn2_w1[l], ffn2_w3[l], ffn2_w2[l], row(final_norm),
                     batch_major_in=False, final_norm=True, bsz=bsz, seq=seq)
```

```python
import functools

import jax
import jax.numpy as jnp
from jax import lax
from jax.experimental import pallas as pl
from jax.experimental.pallas import tpu as pltpu

D_MODEL = 1024
S5_GROUP = 16
S5_WIDTH = 512
S5_GROUPS = 32
S5_STATE = 64
GLA_HEADS = 4
GLA_DV = 128
GLA_DK = 64
GLA_KEY_WIDTH = 256
GLA_VAL_WIDTH = 512
GLA_GATE_RANK = 16
GLA_TAU = 16.0
GLA_CHUNK = 64
D_FF = 2816
EPS = 1e-6
IN_SIZES = (S5_WIDTH, GLA_KEY_WIDTH, GLA_KEY_WIDTH, GLA_VAL_WIDTH, GLA_VAL_WIDTH,
            GLA_GATE_RANK, D_MODEL, D_MODEL)

LANES = 128
SUBLANES = 8
S5_BLOCKS = 4
S5_BLOCK_GROUPS = S5_GROUPS // S5_BLOCKS
S5_BLOCK_STATE = S5_BLOCK_GROUPS * S5_STATE
A_LOW_PAD = LANES
SLABS = D_MODEL // LANES
FFN_STEPS = 64
MIX_STEPS = 128
FF_CHUNKS = (768, 768, 768, 512)
assert sum(FF_CHUNKS) == D_FF
FF_PREP_COLS = 256
FF_PREP_STEPS = D_FF // FF_PREP_COLS
assert FF_PREP_STEPS * FF_PREP_COLS == D_FF
VMEM_LIMIT = 56 * 1024 * 1024

BF16 = jnp.bfloat16
F32 = jnp.float32

C_S5 = 0
C_Q = C_S5 + S5_WIDTH
C_K = C_Q + GLA_KEY_WIDTH
C_V = C_K + GLA_KEY_WIDTH
C_R = C_V + GLA_VAL_WIDTH
MIX_COLS = C_R + GLA_VAL_WIDTH
C_GS5 = 0
C_GGLA = C_GS5 + D_MODEL
GATE_COLS = C_GGLA + D_MODEL


def _dot(a, b):
    return jnp.dot(a, b, preferred_element_type=F32)


def _dot_nt(a, b):
    return lax.dot_general(a, b, (((1,), (1,)), ((), ())), preferred_element_type=F32)


def _dot_tn(a, b):
    return lax.dot_general(a, b, (((0,), (0,)), ((), ())), preferred_element_type=F32)


def _inv_rms(x):
    return lax.rsqrt(jnp.mean(x * x, axis=-1, keepdims=True) + EPS)


def _rms_norm(x, g):
    return x * _inv_rms(x) * g


def _const_spec(shape):
    nd = len(shape)
    return pl.BlockSpec(shape, lambda *_: (0,) * nd, pipeline_mode=pl.Buffered(1))


def _ffn_body(x_ref, g_ref, w1_ref, w3_ref, w2_ref, gf_ref, o_ref, w1b, w3b, w2b, *,
              batch_major_in, final_norm, bsz, steps):
    step = pl.program_id(0)
    for c in range(FF_PREP_STEPS):
        @pl.when(step == c)
        def _(c=c):
            cols = slice(c * FF_PREP_COLS, (c + 1) * FF_PREP_COLS)
            gain = g_ref[...]
            w1b[:, cols] = (gain * w1_ref[...]).astype(BF16)
            w3b[:, cols] = (gain * w3_ref[...]).astype(BF16)
            w2b[cols, :] = w2_ref[...].astype(BF16)

    @pl.when(step >= FF_PREP_STEPS)
    def _():
        _ffn_tile(x_ref, w1b, w3b, w2b, gf_ref, o_ref, batch_major_in=batch_major_in,
                  final_norm=final_norm, bsz=bsz, steps=steps)


def _ffn_tile(x_ref, w1_ref, w3_ref, w2_ref, gf_ref, o_ref, *,
              batch_major_in, final_norm, bsz, steps):
    rows = bsz * steps
    if batch_major_in:
        x = x_ref[...].reshape(rows, D_MODEL)
    else:
        x = jnp.concatenate(
            [jnp.concatenate([x_ref[j, pl.ds(bi, steps, stride=bsz), :] for j in range(SLABS)],
                             axis=-1) for bi in range(bsz)], axis=0)
    xb = x.astype(BF16)
    inv = _inv_rms(x)
    acc = None
    col = 0
    for width in FF_CHUNKS:
        a = _dot(xb, w1_ref[:, col:col + width]) * inv
        b = _dot(xb, w3_ref[:, col:col + width]) * inv
        hmid = (a * jax.nn.sigmoid(a) * b).astype(BF16)
        part = _dot(hmid, w2_ref[col:col + width, :])
        acc = part if acc is None else acc + part
        col += width
    y = x + 0.5 * acc
    if final_norm:
        y = _rms_norm(y, gf_ref[...])
    if batch_major_in:
        for bi in range(bsz):
            for j in range(SLABS):
                o_ref[j, pl.ds(bi, steps, stride=bsz), :] = (
                    y[bi * steps:(bi + 1) * steps, j * LANES:(j + 1) * LANES])
    else:
        o_ref[...] = y.reshape(bsz, steps, D_MODEL)


def _ffn_call(x, g, w1, w3, w2, gf, *, batch_major_in, final_norm, bsz, seq):
    steps = min(FFN_STEPS, seq)
    bm_block, tm_block = (bsz, steps, D_MODEL), (SLABS, steps * bsz, LANES)
    in_block, out_block = (bm_block, tm_block) if batch_major_in else (tm_block, bm_block)
    out_shape = (SLABS, seq * bsz, LANES) if batch_major_in else (bsz, seq, D_MODEL)
    tile = lambda i: (0, jnp.maximum(i - FF_PREP_STEPS, 0), 0)
    prep = lambda i: jnp.minimum(i, FF_PREP_STEPS - 1)
    return pl.pallas_call(
        functools.partial(_ffn_body, batch_major_in=batch_major_in, final_norm=final_norm,
                          bsz=bsz, steps=steps),
        grid=(FF_PREP_STEPS + seq // steps,),
        in_specs=[pl.BlockSpec(in_block, tile),
                  _const_spec((D_MODEL, 1)),
                  pl.BlockSpec((D_MODEL, FF_PREP_COLS), lambda i: (0, prep(i))),
                  pl.BlockSpec((D_MODEL, FF_PREP_COLS), lambda i: (0, prep(i))),
                  pl.BlockSpec((FF_PREP_COLS, D_MODEL), lambda i: (prep(i), 0)),
                  _const_spec((1, D_MODEL))],
        out_specs=pl.BlockSpec(out_block, tile),
        out_shape=jax.ShapeDtypeStruct(out_shape, F32),
        scratch_shapes=[pltpu.VMEM((D_MODEL, D_FF), BF16), pltpu.VMEM((D_MODEL, D_FF), BF16),
                        pltpu.VMEM((D_FF, D_MODEL), BF16)],
        compiler_params=pltpu.CompilerParams(
            dimension_semantics=("arbitrary",), vmem_limit_bytes=VMEM_LIMIT),
        name="ffn_final" if final_norm else "ffn",
    )(x, g, w1, w3, w2, gf)


def _mixer_body(h_ref, w_mix_ref, w_gate_ref, w_alow_ref, a_re_ref, a_im_ref, wb_ref, wc_ref,
                d_ref, glu_w_ref, glu_b_ref, a_up_w_ref, a_up_b_ref, gnorm_ref,
                p_s5_ref, p_gla_ref, w_out_ref, o_ref,
                s5_state, gla_state, bu_buf, loga_buf, bcum_buf, *, bsz, steps, chunk):
    rows = bsz * steps
    crows = bsz * chunk

    @pl.when(pl.program_id(0) == 0)
    def _():
        s5_state[...] = jnp.zeros_like(s5_state)
        gla_state[...] = jnp.zeros_like(gla_state)

    h = jnp.concatenate([h_ref[j] for j in range(SLABS)], axis=-1)
    u = h.astype(BF16)
    inv = _inv_rms(h)

    def proj(col, width, w_ref=w_mix_ref):
        return _dot_nt(u, w_ref[col:col + width, :]) * inv

    s5_in = proj(C_S5, S5_WIDTH)
    s5_in_bf = s5_in.astype(BF16)
    y_parts = []
    for m in range(S5_BLOCKS):
        buf = bu_buf.at[m]
        buf[...] = _dot(s5_in_bf[:, m * LANES:(m + 1) * LANES], wb_ref[m])
        a_re = jnp.broadcast_to(a_re_ref[m], (bsz, S5_BLOCK_STATE))
        a_im = jnp.broadcast_to(a_im_ref[m], (bsz, S5_BLOCK_STATE))
        s_re = s5_state[m, :, :S5_BLOCK_STATE]
        s_im = s5_state[m, :, S5_BLOCK_STATE:]
        for t in range(steps):
            r0 = t * bsz
            n_re = a_re * s_re - a_im * s_im + buf[r0:r0 + bsz, :S5_BLOCK_STATE]
            n_im = a_re * s_im + a_im * s_re + buf[r0:r0 + bsz, S5_BLOCK_STATE:]
            buf[r0:r0 + bsz, :S5_BLOCK_STATE] = n_re
            buf[r0:r0 + bsz, S5_BLOCK_STATE:] = n_im
            s_re, s_im = n_re, n_im
        s5_state[m, :, :S5_BLOCK_STATE] = s_re
        s5_state[m, :, S5_BLOCK_STATE:] = s_im
        y_parts.append(_dot(buf[...].astype(BF16), wc_ref[m]))
    y = jnp.concatenate(y_parts, axis=-1) + d_ref[...] * s5_in
    z = jax.nn.gelu(y)
    y_s5 = z * jax.nn.sigmoid(_dot(z.astype(BF16), glu_w_ref[...]) + glu_b_ref[...])

    a_low = proj(0, A_LOW_PAD, w_alow_ref).astype(BF16)
    loga_buf[...] = jax.nn.log_sigmoid(_dot(a_low, a_up_w_ref[...]) + a_up_b_ref[...]) / GLA_TAU

    q = proj(C_Q, GLA_KEY_WIDTH) * (GLA_DK ** -0.5)
    k = proj(C_K, GLA_KEY_WIDTH)
    v = proj(C_V, GLA_VAL_WIDTH).astype(BF16)

    row = lax.broadcasted_iota(jnp.int32, (crows, crows), 0)
    col = lax.broadcasted_iota(jnp.int32, (crows, crows), 1)
    causal = ((row % bsz) == (col % bsz)) & (col <= row)
    own_batch = (col // GLA_DK) == (row % bsz)
    own_batch8 = own_batch[:bsz]
    key_lane = lax.broadcasted_iota(jnp.int32, (crows, GLA_KEY_WIDTH), 1) // GLA_DK

    def expand(xh):
        rep = xh + pltpu.roll(xh, 2 * GLA_DK, 1)
        rep = rep + pltpu.roll(rep, GLA_DK, 1)
        return jnp.concatenate([rep, rep], axis=-1)

    o_chunks = []
    for c in range(steps // chunk):
        c0 = c * crows
        b_last = jnp.zeros((bsz, GLA_KEY_WIDTH), F32)
        for t in range(chunk):
            r0 = c0 + t * bsz
            b_last = b_last + loga_buf[r0:r0 + bsz, :]
            bcum_buf[r0:r0 + bsz, :] = b_last
        bcum = bcum_buf[c0:c0 + crows, :]
        b_last_rows = jnp.broadcast_to(b_last[None], (chunk, bsz, GLA_KEY_WIDTH)).reshape(
            crows, GLA_KEY_WIDTH)
        k_c = k[c0:c0 + crows]
        q_t = q[c0:c0 + crows] * jnp.exp(bcum)
        k_t = (k_c * jnp.exp(-bcum)).astype(BF16)
        k_end = k_c * jnp.exp(b_last_rows - bcum)
        o_parts = []
        for hd in range(GLA_HEADS):
            head = key_lane == hd
            q_h = jnp.where(head, q_t, 0.0)
            scores = _dot_nt(q_h.astype(BF16), k_t)
            p = jnp.where(causal, scores, 0.0).astype(BF16)
            v_h = v[c0:c0 + crows, hd * GLA_DV:(hd + 1) * GLA_DV]
            o_h = _dot(p, v_h)
            q_exp = jnp.where(own_batch, expand(q_h), 0.0).astype(BF16)
            s_prev = gla_state[hd]
            o_h = o_h + _dot_nt(q_exp, s_prev.astype(BF16))
            k_exp = jnp.where(own_batch, expand(jnp.where(head, k_end, 0.0)), 0.0).astype(BF16)
            d_state = _dot_tn(v_h, k_exp)
            bl = expand(jnp.where(head[:bsz], b_last, 0.0))
            decay = jnp.exp(jnp.sum(jnp.where(own_batch8, bl, 0.0), axis=0, keepdims=True))
            gla_state[hd] = decay * s_prev + d_state
            o_parts.append(o_h * lax.rsqrt(jnp.mean(o_h * o_h, axis=-1, keepdims=True) + EPS))
        o_chunks.append(jnp.concatenate(o_parts, axis=-1))
    o = jnp.concatenate(o_chunks, axis=0) * gnorm_ref[...]
    r = proj(C_R, GLA_VAL_WIDTH)
    y_gla = o * (r * jax.nn.sigmoid(r))

    g_s5 = jax.nn.sigmoid(proj(C_GS5, D_MODEL, w_gate_ref))
    g_gla = jax.nn.sigmoid(proj(C_GGLA, D_MODEL, w_gate_ref))
    merged = (g_s5 * _dot(y_s5.astype(BF16), p_s5_ref[...])
              + g_gla * _dot(y_gla.astype(BF16), p_gla_ref[...]))
    out = h + _dot(merged.astype(BF16), w_out_ref[...])
    for j in range(SLABS):
        o_ref[j] = out[:, j * LANES:(j + 1) * LANES]


def _mixer_call(h, w_mix, w_gate, w_alow, a_re, a_im, wb, wc, d_skip, glu_w, glu_b, a_up_w,
                a_up_b, gnorm, p_s5, p_gla, w_out, *, bsz, seq):
    steps = min(MIX_STEPS, seq)
    rows = bsz * steps
    args = (w_mix, w_gate, w_alow, a_re, a_im, wb, wc, d_skip, glu_w, glu_b, a_up_w, a_up_b,
            gnorm, p_s5, p_gla, w_out)
    tile_spec = pl.BlockSpec((SLABS, rows, LANES), lambda i: (0, i, 0))
    return pl.pallas_call(
        functools.partial(_mixer_body, bsz=bsz, steps=steps, chunk=GLA_CHUNK),
        grid=(seq // steps,),
        in_specs=[tile_spec] + [_const_spec(a.shape) for a in args],
        out_specs=tile_spec,
        out_shape=jax.ShapeDtypeStruct((SLABS, seq * bsz, LANES), F32),
        scratch_shapes=[
            pltpu.VMEM((S5_BLOCKS, bsz, 2 * S5_BLOCK_STATE), F32),
            pltpu.VMEM((GLA_HEADS, GLA_DV, bsz * GLA_DK), F32),
            pltpu.VMEM((S5_BLOCKS, rows, 2 * S5_BLOCK_STATE), F32),
            pltpu.VMEM((rows, GLA_KEY_WIDTH), F32),
            pltpu.VMEM((rows, GLA_KEY_WIDTH), F32),
        ],
        compiler_params=pltpu.CompilerParams(
            dimension_semantics=("arbitrary",), vmem_limit_bytes=VMEM_LIMIT),
        name="mixer",
    )(h, *args)


def _s5_discretize_body(lam_re_ref, lam_im_ref, log_dt_ref, b_re_ref, b_im_ref,
                        ar_ref, ai_ref, bbar_re_ref, bbar_im_ref):
    lam_re = lam_re_ref[...]
    lam_im = lam_im_ref[...]
    dt = jnp.exp(log_dt_ref[...])
    mag = jnp.exp(lam_re * dt)
    ar = mag * jnp.cos(lam_im * dt)
    ai = mag * jnp.sin(lam_im * dt)
    den = lam_re * lam_re + lam_im * lam_im
    nr = ar - 1.0
    fr = ((nr * lam_re + ai * lam_im) / den)[:, None, :]
    fi = ((ai * lam_re - nr * lam_im) / den)[:, None, :]
    ar_ref[...] = ar
    ai_ref[...] = ai
    b_re = b_re_ref[...]
    b_im = b_im_ref[...]
    bbar_re_ref[...] = fr * b_re - fi * b_im
    bbar_im_ref[...] = fr * b_im + fi * b_re


def _s5_discretize(lam_re, lam_im, log_dt, b_re, b_im):
    gp = jax.ShapeDtypeStruct((S5_GROUPS, S5_STATE), F32)
    ghp = jax.ShapeDtypeStruct((S5_GROUPS, S5_GROUP, S5_STATE), F32)
    return pl.pallas_call(
        _s5_discretize_body, out_shape=(gp, gp, ghp, ghp), name="s5_discretize",
    )(lam_re, lam_im, log_dt.reshape(S5_GROUPS, 1),
      jnp.swapaxes(b_re, 1, 2), jnp.swapaxes(b_im, 1, 2))


def _pack_s5(ar, ai, bbar_re, bbar_im, c_re, c_im):
    eye = jnp.eye(S5_BLOCK_GROUPS, dtype=F32)
    nb, ng, ns, nh = S5_BLOCKS, S5_BLOCK_GROUPS, S5_STATE, S5_GROUP
    bb = jnp.stack([bbar_re, bbar_im]).reshape(2, nb, ng, nh, ns)
    wb = jnp.einsum('rmghp,gk->mghrkp', bb, eye).reshape(nb, ng * nh, 2 * ng * ns)
    cc = jnp.stack([c_re, -c_im]).reshape(2, nb, ng, nh, ns)
    wc = jnp.einsum('rmghp,gk->mrgpkh', cc, eye).reshape(nb, 2 * ng * ns, ng * nh)
    a_re = ar.reshape(nb, 1, ng * ns)
    a_im = ai.reshape(nb, 1, ng * ns)
    return a_re, a_im, wb.astype(BF16), wc.astype(BF16)


def kernel(x, ffn1_norm, ffn1_w1, ffn1_w3, ffn1_w2, mix_norm, w_in, s5_lambda_re, s5_lambda_im, s5_log_dt, s5_b_re, s5_b_im, s5_c_re, s5_c_im, s5_d, s5_glu_w, s5_glu_b, gla_a_up_w, gla_a_up_b, gla_out_norm, proj_s5, proj_gla, w_out, ffn2_norm, ffn2_w1, ffn2_w3, ffn2_w2, final_norm):
    bsz, seq, _ = x.shape
    assert bsz == SUBLANES and seq % GLA_CHUNK == 0 and ffn1_norm.shape[0] == 1
    assert seq % min(FFN_STEPS, seq) == 0 and seq % min(MIX_STEPS, seq) == 0
    assert MIX_STEPS % GLA_CHUNK == 0
    l = 0
    row = lambda a: a.reshape(1, -1).astype(F32)

    col = lambda a: a.reshape(-1, 1).astype(F32)
    h = _ffn_call(x, col(ffn1_norm[l]), ffn1_w1[l], ffn1_w3[l], ffn1_w2[l], row(final_norm),
                  batch_major_in=True, final_norm=False, bsz=bsz, seq=seq)

    assert IN_SIZES[5] == GLA_GATE_RANK and sum(IN_SIZES[:5]) == MIX_COLS
    w_in_t = (jnp.swapaxes(w_in, 1, 2)[l] * mix_norm[l].astype(F32)[None, :]).astype(BF16)
    gate_start = MIX_COLS + GLA_GATE_RANK
    w_mix = w_in_t[:MIX_COLS]
    w_gate = w_in_t[gate_start:gate_start + GATE_COLS]
    w_alow = jnp.pad(w_in_t[MIX_COLS:gate_start], ((0, A_LOW_PAD - GLA_GATE_RANK), (0, 0)))
    a_up_w = jnp.concatenate(
        [gla_a_up_w[l], jnp.zeros((A_LOW_PAD - GLA_GATE_RANK, GLA_KEY_WIDTH), F32)],
        axis=0).astype(BF16)
    ar, ai, bbar_re, bbar_im = _s5_discretize(
        s5_lambda_re[l], s5_lambda_im[l], s5_log_dt[l], s5_b_re[l], s5_b_im[l])
    a_re, a_im, wb, wc = _pack_s5(ar, ai, bbar_re, bbar_im, s5_c_re[l], s5_c_im[l])

    h = _mixer_call(h, w_mix, w_gate, w_alow, a_re, a_im, wb, wc, row(s5_d[l]),
                    s5_glu_w[l].astype(BF16), row(s5_glu_b[l]), a_up_w, row(gla_a_up_b[l]),
                    row(gla_out_norm[l]), proj_s5[l].astype(BF16), proj_gla[l].astype(BF16),
                    w_out[l].astype(BF16), bsz=bsz, seq=seq)

    return _ffn_call(h, col(ffn2_norm[l]), ffn2_w1[l], ffn2_w3[l], ffn2_w2[l], row(final_norm),
                     batch_major_in=False, final_norm=True, bsz=bsz, seq=seq)
```

```python
import functools

import jax
import jax.numpy as jnp
from jax import lax
from jax.experimental import pallas as pl
from jax.experimental.pallas import tpu as pltpu

D_MODEL = 1024
S5_GROUP = 16
S5_WIDTH = 512
S5_GROUPS = 32
S5_STATE = 64
GLA_HEADS = 4
GLA_DV = 128
GLA_DK = 64
GLA_KEY_WIDTH = 256
GLA_VAL_WIDTH = 512
GLA_GATE_RANK = 16
GLA_TAU = 16.0
GLA_CHUNK = 64
D_FF = 2816
EPS = 1e-6
IN_SIZES = (S5_WIDTH, GLA_KEY_WIDTH, GLA_KEY_WIDTH, GLA_VAL_WIDTH, GLA_VAL_WIDTH,
            GLA_GATE_RANK, D_MODEL, D_MODEL)

LANES = 128
SUBLANES = 8
S5_BLOCKS = 4
S5_BLOCK_GROUPS = S5_GROUPS // S5_BLOCKS
S5_BLOCK_STATE = S5_BLOCK_GROUPS * S5_STATE
A_LOW_PAD = LANES
SLABS = D_MODEL // LANES
FFN_STEPS = 128
MIX_STEPS = 128
FF_CHUNKS = (768, 768, 768, 512)
assert sum(FF_CHUNKS) == D_FF
FF_PREP_COLS = 256
FF_PREP_STEPS = D_FF // FF_PREP_COLS
assert FF_PREP_STEPS * FF_PREP_COLS == D_FF
VMEM_LIMIT = 56 * 1024 * 1024

BF16 = jnp.bfloat16
F32 = jnp.float32

C_S5 = 0
C_Q = C_S5 + S5_WIDTH
C_K = C_Q + GLA_KEY_WIDTH
C_V = C_K + GLA_KEY_WIDTH
C_R = C_V + GLA_VAL_WIDTH
MIX_COLS = C_R + GLA_VAL_WIDTH
C_GS5 = 0
C_GGLA = C_GS5 + D_MODEL
GATE_COLS = C_GGLA + D_MODEL


def _dot(a, b):
    return jnp.dot(a, b, preferred_element_type=F32)


def _dot_nt(a, b):
    return lax.dot_general(a, b, (((1,), (1,)), ((), ())), preferred_element_type=F32)


def _dot_tn(a, b):
    return lax.dot_general(a, b, (((0,), (0,)), ((), ())), preferred_element_type=F32)


def _inv_rms(x):
    return lax.rsqrt(jnp.mean(x * x, axis=-1, keepdims=True) + EPS)


def _rms_norm(x, g):
    return x * _inv_rms(x) * g


def _const_spec(shape):
    nd = len(shape)
    return pl.BlockSpec(shape, lambda *_: (0,) * nd, pipeline_mode=pl.Buffered(1))


def _ffn_body(x_ref, g_ref, w1_ref, w3_ref, w2_ref, gf_ref, o_ref, w1b, w3b, w2b, *,
              batch_major_in, final_norm, bsz, steps):
    step = pl.program_id(0)
    for c in range(FF_PREP_STEPS):
        @pl.when(step == c)
        def _(c=c):
            cols = slice(c * FF_PREP_COLS, (c + 1) * FF_PREP_COLS)
            gain = g_ref[...]
            w1b[:, cols] = (gain * w1_ref[...]).astype(BF16)
            w3b[:, cols] = (gain * w3_ref[...]).astype(BF16)
            w2b[cols, :] = w2_ref[...].astype(BF16)

    @pl.when(step >= FF_PREP_STEPS)
    def _():
        _ffn_tile(x_ref, w1b, w3b, w2b, gf_ref, o_ref, batch_major_in=batch_major_in,
                  final_norm=final_norm, bsz=bsz, steps=steps)


def _ffn_tile(x_ref, w1_ref, w3_ref, w2_ref, gf_ref, o_ref, *,
              batch_major_in, final_norm, bsz, steps):
    rows = bsz * steps
    if batch_major_in:
        x = x_ref[...].reshape(rows, D_MODEL)
    else:
        x = jnp.concatenate(
            [jnp.concatenate([x_ref[j, pl.ds(bi, steps, stride=bsz), :] for j in range(SLABS)],
                             axis=-1) for bi in range(bsz)], axis=0)
    xb = x.astype(BF16)
    inv = _inv_rms(x)
    acc = None
    col = 0
    for width in FF_CHUNKS:
        a = _dot(xb, w1_ref[:, col:col + width]) * inv
        b = _dot(xb, w3_ref[:, col:col + width]) * inv
        hmid = (a * jax.nn.sigmoid(a) * b).astype(BF16)
        part = _dot(hmid, w2_ref[col:col + width, :])
        acc = part if acc is None else acc + part
        col += width
    y = x + 0.5 * acc
    if final_norm:
        y = _rms_norm(y, gf_ref[...])
    if batch_major_in:
        for bi in range(bsz):
            for j in range(SLABS):
                o_ref[j, pl.ds(bi, steps, stride=bsz), :] = (
                    y[bi * steps:(bi + 1) * steps, j * LANES:(j + 1) * LANES])
    else:
        o_ref[...] = y.reshape(bsz, steps, D_MODEL)


def _ffn_call(x, g, w1, w3, w2, gf, *, batch_major_in, final_norm, bsz, seq):
    steps = min(FFN_STEPS, seq)
    bm_block, tm_block = (bsz, steps, D_MODEL), (SLABS, steps * bsz, LANES)
    in_block, out_block = (bm_block, tm_block) if batch_major_in else (tm_block, bm_block)
    out_shape = (SLABS, seq * bsz, LANES) if batch_major_in else (bsz, seq, D_MODEL)
    tile = lambda i: (0, jnp.maximum(i - FF_PREP_STEPS, 0), 0)
    prep = lambda i: jnp.minimum(i, FF_PREP_STEPS - 1)
    return pl.pallas_call(
        functools.partial(_ffn_body, batch_major_in=batch_major_in, final_norm=final_norm,
                          bsz=bsz, steps=steps),
        grid=(FF_PREP_STEPS + seq // steps,),
        in_specs=[pl.BlockSpec(in_block, tile),
                  _const_spec((D_MODEL, 1)),
                  pl.BlockSpec((D_MODEL, FF_PREP_COLS), lambda i: (0, prep(i))),
                  pl.BlockSpec((D_MODEL, FF_PREP_COLS), lambda i: (0, prep(i))),
                  pl.BlockSpec((FF_PREP_COLS, D_MODEL), lambda i: (prep(i), 0)),
                  _const_spec((1, D_MODEL))],
        out_specs=pl.BlockSpec(out_block, tile),
        out_shape=jax.ShapeDtypeStruct(out_shape, F32),
        scratch_shapes=[pltpu.VMEM((D_MODEL, D_FF), BF16), pltpu.VMEM((D_MODEL, D_FF), BF16),
                        pltpu.VMEM((D_FF, D_MODEL), BF16)],
        compiler_params=pltpu.CompilerParams(
            dimension_semantics=("arbitrary",), vmem_limit_bytes=VMEM_LIMIT),
        name="ffn_final" if final_norm else "ffn",
    )(x, g, w1, w3, w2, gf)


def _mixer_body(h_ref, g_ref, w_mix_ref, w_gate_ref, w_alow_ref, a_re_ref, a_im_ref, wb_ref, wc_ref,
                d_ref, glu_w_ref, glu_b_ref, a_up_w_ref, a_up_b_ref, gnorm_ref,
                p_s5_ref, p_gla_ref, w_out_ref, o_ref,
                s5_state, gla_state, bu_buf, loga_buf, bcum_buf, *, bsz, steps, chunk):
    rows = bsz * steps
    crows = bsz * chunk

    @pl.when(pl.program_id(0) == 0)
    def _():
        s5_state[...] = jnp.zeros_like(s5_state)
        gla_state[...] = jnp.zeros_like(gla_state)

    h = jnp.concatenate([h_ref[j] for j in range(SLABS)], axis=-1)
    u = (h * g_ref[...]).astype(BF16)
    inv = _inv_rms(h)

    def proj(col, width, w_ref=w_mix_ref):
        return _dot_nt(u, w_ref[col:col + width, :]) * inv

    s5_in = proj(C_S5, S5_WIDTH)
    s5_in_bf = s5_in.astype(BF16)
    y_parts = []
    for m in range(S5_BLOCKS):
        buf = bu_buf.at[m]
        buf[...] = _dot(s5_in_bf[:, m * LANES:(m + 1) * LANES], wb_ref[m])
        a_re = jnp.broadcast_to(a_re_ref[m], (bsz, S5_BLOCK_STATE))
        a_im = jnp.broadcast_to(a_im_ref[m], (bsz, S5_BLOCK_STATE))
        s_re = s5_state[m, :, :S5_BLOCK_STATE]
        s_im = s5_state[m, :, S5_BLOCK_STATE:]
        for t in range(steps):
            r0 = t * bsz
            n_re = a_re * s_re - a_im * s_im + buf[r0:r0 + bsz, :S5_BLOCK_STATE]
            n_im = a_re * s_im + a_im * s_re + buf[r0:r0 + bsz, S5_BLOCK_STATE:]
            buf[r0:r0 + bsz, :S5_BLOCK_STATE] = n_re
            buf[r0:r0 + bsz, S5_BLOCK_STATE:] = n_im
            s_re, s_im = n_re, n_im
        s5_state[m, :, :S5_BLOCK_STATE] = s_re
        s5_state[m, :, S5_BLOCK_STATE:] = s_im
        y_parts.append(_dot(buf[...].astype(BF16), wc_ref[m]))
    y = jnp.concatenate(y_parts, axis=-1) + d_ref[...] * s5_in
    z = jax.nn.gelu(y)
    y_s5 = z * jax.nn.sigmoid(_dot(z.astype(BF16), glu_w_ref[...]) + glu_b_ref[...])

    a_low = proj(0, A_LOW_PAD, w_alow_ref).astype(BF16)
    loga_buf[...] = jax.nn.log_sigmoid(_dot(a_low, a_up_w_ref[...]) + a_up_b_ref[...]) / GLA_TAU

    q = proj(C_Q, GLA_KEY_WIDTH) * (GLA_DK ** -0.5)
    k = proj(C_K, GLA_KEY_WIDTH)
    v = proj(C_V, GLA_VAL_WIDTH).astype(BF16)

    row = lax.broadcasted_iota(jnp.int32, (crows, crows), 0)
    col = lax.broadcasted_iota(jnp.int32, (crows, crows), 1)
    causal = ((row % bsz) == (col % bsz)) & (col <= row)
    own_batch = (col // GLA_DK) == (row % bsz)
    own_batch8 = own_batch[:bsz]
    key_lane = lax.broadcasted_iota(jnp.int32, (crows, GLA_KEY_WIDTH), 1) // GLA_DK

    def expand(xh):
        rep = xh + pltpu.roll(xh, 2 * GLA_DK, 1)
        rep = rep + pltpu.roll(rep, GLA_DK, 1)
        return jnp.concatenate([rep, rep], axis=-1)

    o_chunks = []
    for c in range(steps // chunk):
        c0 = c * crows
        b_last = jnp.zeros((bsz, GLA_KEY_WIDTH), F32)
        for t in range(chunk):
            r0 = c0 + t * bsz
            b_last = b_last + loga_buf[r0:r0 + bsz, :]
            bcum_buf[r0:r0 + bsz, :] = b_last
        bcum = bcum_buf[c0:c0 + crows, :]
        b_last_rows = jnp.broadcast_to(b_last[None], (chunk, bsz, GLA_KEY_WIDTH)).reshape(
            crows, GLA_KEY_WIDTH)
        k_c = k[c0:c0 + crows]
        q_t = q[c0:c0 + crows] * jnp.exp(bcum)
        k_t = (k_c * jnp.exp(-bcum)).astype(BF16)
        k_end = k_c * jnp.exp(b_last_rows - bcum)
        o_parts = []
        for hd in range(GLA_HEADS):
            head = key_lane == hd
            q_h = jnp.where(head, q_t, 0.0)
            scores = _dot_nt(q_h.astype(BF16), k_t)
            p = jnp.where(causal, scores, 0.0).astype(BF16)
            v_h = v[c0:c0 + crows, hd * GLA_DV:(hd + 1) * GLA_DV]
            o_h = _dot(p, v_h)
            q_exp = jnp.where(own_batch, expand(q_h), 0.0).astype(BF16)
            s_prev = gla_state[hd]
            o_h = o_h + _dot_nt(q_exp, s_prev.astype(BF16))
            k_exp = jnp.where(own_batch, expand(jnp.where(head, k_end, 0.0)), 0.0).astype(BF16)
            d_state = _dot_tn(v_h, k_exp)
            bl = expand(jnp.where(head[:bsz], b_last, 0.0))
            decay = jnp.exp(jnp.sum(jnp.where(own_batch8, bl, 0.0), axis=0, keepdims=True))
            gla_state[hd] = decay * s_prev + d_state
            o_parts.append(o_h * lax.rsqrt(jnp.mean(o_h * o_h, axis=-1, keepdims=True) + EPS))
        o_chunks.append(jnp.concatenate(o_parts, axis=-1))
    o = jnp.concatenate(o_chunks, axis=0) * gnorm_ref[...]
    r = proj(C_R, GLA_VAL_WIDTH)
    y_gla = o * (r * jax.nn.sigmoid(r))

    g_s5 = jax.nn.sigmoid(proj(C_GS5, D_MODEL, w_gate_ref))
    g_gla = jax.nn.sigmoid(proj(C_GGLA, D_MODEL, w_gate_ref))
    merged = (g_s5 * _dot(y_s5.astype(BF16), p_s5_ref[...])
              + g_gla * _dot(y_gla.astype(BF16), p_gla_ref[...]))
    out = h + _dot(merged.astype(BF16), w_out_ref[...])
    for j in range(SLABS):
        o_ref[j] = out[:, j * LANES:(j + 1) * LANES]


def _mixer_call(h, g, w_mix, w_gate, w_alow, a_re, a_im, wb, wc, d_skip, glu_w, glu_b, a_up_w,
                a_up_b, gnorm, p_s5, p_gla, w_out, *, bsz, seq):
    steps = min(MIX_STEPS, seq)
    rows = bsz * steps
    args = (g, w_mix, w_gate, w_alow, a_re, a_im, wb, wc, d_skip, glu_w, glu_b, a_up_w, a_up_b,
            gnorm, p_s5, p_gla, w_out)
    tile_spec = pl.BlockSpec((SLABS, rows, LANES), lambda i: (0, i, 0))
    return pl.pallas_call(
        functools.partial(_mixer_body, bsz=bsz, steps=steps, chunk=GLA_CHUNK),
        grid=(seq // steps,),
        in_specs=[tile_spec] + [_const_spec(a.shape) for a in args],
        out_specs=tile_spec,
        out_shape=jax.ShapeDtypeStruct((SLABS, seq * bsz, LANES), F32),
        scratch_shapes=[
            pltpu.VMEM((S5_BLOCKS, bsz, 2 * S5_BLOCK_STATE), F32),
            pltpu.VMEM((GLA_HEADS, GLA_DV, bsz * GLA_DK), F32),
            pltpu.VMEM((S5_BLOCKS, rows, 2 * S5_BLOCK_STATE), F32),
            pltpu.VMEM((rows, GLA_KEY_WIDTH), F32),
            pltpu.VMEM((rows, GLA_KEY_WIDTH), F32),
        ],
        compiler_params=pltpu.CompilerParams(
            dimension_semantics=("arbitrary",), vmem_limit_bytes=VMEM_LIMIT),
        name="mixer",
    )(h, *args)


def _s5_discretize_body(lam_re_ref, lam_im_ref, log_dt_ref, b_re_ref, b_im_ref,
                        ar_ref, ai_ref, bbar_re_ref, bbar_im_ref):
    lam_re = lam_re_ref[...]
    lam_im = lam_im_ref[...]
    dt = jnp.exp(log_dt_ref[...])
    mag = jnp.exp(lam_re * dt)
    ar = mag * jnp.cos(lam_im * dt)
    ai = mag * jnp.sin(lam_im * dt)
    den = lam_re * lam_re + lam_im * lam_im
    nr = ar - 1.0
    fr = ((nr * lam_re + ai * lam_im) / den)[:, None, :]
    fi = ((ai * lam_re - nr * lam_im) / den)[:, None, :]
    ar_ref[...] = ar
    ai_ref[...] = ai
    b_re = b_re_ref[...]
    b_im = b_im_ref[...]
    bbar_re_ref[...] = fr * b_re - fi * b_im
    bbar_im_ref[...] = fr * b_im + fi * b_re


def _s5_discretize(lam_re, lam_im, log_dt, b_re, b_im):
    gp = jax.ShapeDtypeStruct((S5_GROUPS, S5_STATE), F32)
    ghp = jax.ShapeDtypeStruct((S5_GROUPS, S5_GROUP, S5_STATE), F32)
    return pl.pallas_call(
        _s5_discretize_body, out_shape=(gp, gp, ghp, ghp), name="s5_discretize",
    )(lam_re, lam_im, log_dt.reshape(S5_GROUPS, 1),
      jnp.swapaxes(b_re, 1, 2), jnp.swapaxes(b_im, 1, 2))


def _pack_s5(ar, ai, bbar_re, bbar_im, c_re, c_im):
    eye = jnp.eye(S5_BLOCK_GROUPS, dtype=F32)
    nb, ng, ns, nh = S5_BLOCKS, S5_BLOCK_GROUPS, S5_STATE, S5_GROUP
    bb = jnp.stack([bbar_re, bbar_im]).reshape(2, nb, ng, nh, ns)
    wb = jnp.einsum('rmghp,gk->mghrkp', bb, eye).reshape(nb, ng * nh, 2 * ng * ns)
    cc = jnp.stack([c_re, -c_im]).reshape(2, nb, ng, nh, ns)
    wc = jnp.einsum('rmghp,gk->mrgpkh', cc, eye).reshape(nb, 2 * ng * ns, ng * nh)
    a_re = ar.reshape(nb, 1, ng * ns)
    a_im = ai.reshape(nb, 1, ng * ns)
    return a_re, a_im, wb.astype(BF16), wc.astype(BF16)


def kernel(x, ffn1_norm, ffn1_w1, ffn1_w3, ffn1_w2, mix_norm, w_in, s5_lambda_re, s5_lambda_im, s5_log_dt, s5_b_re, s5_b_im, s5_c_re, s5_c_im, s5_d, s5_glu_w, s5_glu_b, gla_a_up_w, gla_a_up_b, gla_out_norm, proj_s5, proj_gla, w_out, ffn2_norm, ffn2_w1, ffn2_w3, ffn2_w2, final_norm):
    bsz, seq, _ = x.shape
    assert bsz == SUBLANES and seq % GLA_CHUNK == 0 and ffn1_norm.shape[0] == 1
    assert seq % min(FFN_STEPS, seq) == 0 and seq % min(MIX_STEPS, seq) == 0
    assert MIX_STEPS % GLA_CHUNK == 0
    l = 0
    row = lambda a: a.reshape(1, -1).astype(F32)

    col = lambda a: a.reshape(-1, 1).astype(F32)
    h = _ffn_call(x, col(ffn1_norm[l]), ffn1_w1[l], ffn1_w3[l], ffn1_w2[l], row(final_norm),
                  batch_major_in=True, final_norm=False, bsz=bsz, seq=seq)

    assert IN_SIZES[5] == GLA_GATE_RANK and sum(IN_SIZES[:5]) == MIX_COLS
    w_in_t = jnp.swapaxes(w_in, 1, 2)[l].astype(BF16)
    gate_start = MIX_COLS + GLA_GATE_RANK
    w_mix = w_in_t[:MIX_COLS]
    w_gate = w_in_t[gate_start:gate_start + GATE_COLS]
    w_alow = jnp.pad(w_in_t[MIX_COLS:gate_start], ((0, A_LOW_PAD - GLA_GATE_RANK), (0, 0)))
    a_up_w = jnp.concatenate(
        [gla_a_up_w[l], jnp.zeros((A_LOW_PAD - GLA_GATE_RANK, GLA_KEY_WIDTH), F32)],
        axis=0).astype(BF16)
    ar, ai, bbar_re, bbar_im = _s5_discretize(
        s5_lambda_re[l], s5_lambda_im[l], s5_log_dt[l], s5_b_re[l], s5_b_im[l])
    a_re, a_im, wb, wc = _pack_s5(ar, ai, bbar_re, bbar_im, s5_c_re[l], s5_c_im[l])

    h = _mixer_call(h, row(mix_norm[l]), w_mix, w_gate, w_alow, a_re, a_im, wb, wc, row(s5_d[l]),
                    s5_glu_w[l].astype(BF16), row(s5_glu_b[l]), a_up_w, row(gla_a_up_b[l]),
                    row(gla_out_norm[l]), proj_s5[l].astype(BF16), proj_gla[l].astype(BF16),
                    w_out[l].astype(BF16), bsz=bsz, seq=seq)

    return _ffn_call(h, col(ffn2_norm[l]), ffn2_w1[l], ffn2_w3[l], ffn2_w2[l], row(final_norm),
                     batch_major_in=False, final_norm=True, bsz=bsz, seq=seq)
```

```python
import functools

import jax
import jax.numpy as jnp
from jax import lax
from jax.experimental import pallas as pl
from jax.experimental.pallas import tpu as pltpu

D_MODEL = 1024
S5_GROUP = 16
S5_WIDTH = 512
S5_GROUPS = 32
S5_STATE = 64
GLA_HEADS = 4
GLA_DV = 128
GLA_DK = 64
GLA_KEY_WIDTH = 256
GLA_VAL_WIDTH = 512
GLA_GATE_RANK = 16
GLA_TAU = 16.0
GLA_CHUNK = 64
D_FF = 2816
EPS = 1e-6
IN_SIZES = (S5_WIDTH, GLA_KEY_WIDTH, GLA_KEY_WIDTH, GLA_VAL_WIDTH, GLA_VAL_WIDTH,
            GLA_GATE_RANK, D_MODEL, D_MODEL)

LANES = 128
SUBLANES = 8
S5_BLOCKS = 4
S5_BLOCK_GROUPS = S5_GROUPS // S5_BLOCKS
S5_BLOCK_STATE = S5_BLOCK_GROUPS * S5_STATE
A_LOW_PAD = LANES
SLABS = D_MODEL // LANES
FFN_STEPS = 128
MIX_STEPS = 64
FF_CHUNKS = (768, 768, 768, 512)
assert sum(FF_CHUNKS) == D_FF
FF_PREP_COLS = 256
FF_PREP_STEPS = D_FF // FF_PREP_COLS
assert FF_PREP_STEPS * FF_PREP_COLS == D_FF
VMEM_LIMIT = 56 * 1024 * 1024

BF16 = jnp.bfloat16
F32 = jnp.float32

C_S5 = 0
C_Q = C_S5 + S5_WIDTH
C_K = C_Q + GLA_KEY_WIDTH
C_V = C_K + GLA_KEY_WIDTH
C_R = C_V + GLA_VAL_WIDTH
MIX_COLS = C_R + GLA_VAL_WIDTH
C_GS5 = 0
C_GGLA = C_GS5 + D_MODEL
GATE_COLS = C_GGLA + D_MODEL


def _dot(a, b):
    return jnp.dot(a, b, preferred_element_type=F32)


def _dot_nt(a, b):
    return lax.dot_general(a, b, (((1,), (1,)), ((), ())), preferred_element_type=F32)


def _dot_tn(a, b):
    return lax.dot_general(a, b, (((0,), (0,)), ((), ())), preferred_element_type=F32)


def _inv_rms(x):
    return lax.rsqrt(jnp.mean(x * x, axis=-1, keepdims=True) + EPS)


def _rms_norm(x, g):
    return x * _inv_rms(x) * g


def _const_spec(shape):
    nd = len(shape)
    return pl.BlockSpec(shape, lambda *_: (0,) * nd, pipeline_mode=pl.Buffered(1))


def _ffn_body(x_ref, g_ref, w1_ref, w3_ref, w2_ref, gf_ref, o_ref, w1b, w3b, w2b, *,
              batch_major_in, final_norm, bsz, steps):
    step = pl.program_id(0)
    for c in range(FF_PREP_STEPS):
        @pl.when(step == c)
        def _(c=c):
            cols = slice(c * FF_PREP_COLS, (c + 1) * FF_PREP_COLS)
            gain = g_ref[...]
            w1b[:, cols] = (gain * w1_ref[...]).astype(BF16)
            w3b[:, cols] = (gain * w3_ref[...]).astype(BF16)
            w2b[cols, :] = w2_ref[...].astype(BF16)

    @pl.when(step >= FF_PREP_STEPS)
    def _():
        _ffn_tile(x_ref, w1b, w3b, w2b, gf_ref, o_ref, batch_major_in=batch_major_in,
                  final_norm=final_norm, bsz=bsz, steps=steps)


def _ffn_tile(x_ref, w1_ref, w3_ref, w2_ref, gf_ref, o_ref, *,
              batch_major_in, final_norm, bsz, steps):
    rows = bsz * steps
    if batch_major_in:
        x = x_ref[...].reshape(rows, D_MODEL)
    else:
        x = jnp.concatenate(
            [jnp.concatenate([x_ref[j, pl.ds(bi, steps, stride=bsz), :] for j in range(SLABS)],
                             axis=-1) for bi in range(bsz)], axis=0)
    xb = x.astype(BF16)
    inv = _inv_rms(x)
    acc = None
    col = 0
    for width in FF_CHUNKS:
        a = _dot(xb, w1_ref[:, col:col + width]) * inv
        b = _dot(xb, w3_ref[:, col:col + width]) * inv
        hmid = (a * jax.nn.sigmoid(a) * b).astype(BF16)
        part = _dot(hmid, w2_ref[col:col + width, :])
        acc = part if acc is None else acc + part
        col += width
    y = x + 0.5 * acc
    if final_norm:
        y = _rms_norm(y, gf_ref[...])
    if batch_major_in:
        for bi in range(bsz):
            for j in range(SLABS):
                o_ref[j, pl.ds(bi, steps, stride=bsz), :] = (
                    y[bi * steps:(bi + 1) * steps, j * LANES:(j + 1) * LANES])
    else:
        o_ref[...] = y.reshape(bsz, steps, D_MODEL)


def _ffn_call(x, g, w1, w3, w2, gf, *, batch_major_in, final_norm, bsz, seq):
    steps = min(FFN_STEPS, seq)
    bm_block, tm_block = (bsz, steps, D_MODEL), (SLABS, steps * bsz, LANES)
    in_block, out_block = (bm_block, tm_block) if batch_major_in else (tm_block, bm_block)
    out_shape = (SLABS, seq * bsz, LANES) if batch_major_in else (bsz, seq, D_MODEL)
    tile = lambda i: (0, jnp.maximum(i - FF_PREP_STEPS, 0), 0)
    prep = lambda i: jnp.minimum(i, FF_PREP_STEPS - 1)
    return pl.pallas_call(
        functools.partial(_ffn_body, batch_major_in=batch_major_in, final_norm=final_norm,
                          bsz=bsz, steps=steps),
        grid=(FF_PREP_STEPS + seq // steps,),
        in_specs=[pl.BlockSpec(in_block, tile),
                  _const_spec((D_MODEL, 1)),
                  pl.BlockSpec((D_MODEL, FF_PREP_COLS), lambda i: (0, prep(i))),
                  pl.BlockSpec((D_MODEL, FF_PREP_COLS), lambda i: (0, prep(i))),
                  pl.BlockSpec((FF_PREP_COLS, D_MODEL), lambda i: (prep(i), 0)),
                  _const_spec((1, D_MODEL))],
        out_specs=pl.BlockSpec(out_block, tile),
        out_shape=jax.ShapeDtypeStruct(out_shape, F32),
        scratch_shapes=[pltpu.VMEM((D_MODEL, D_FF), BF16), pltpu.VMEM((D_MODEL, D_FF), BF16),
                        pltpu.VMEM((D_FF, D_MODEL), BF16)],
        compiler_params=pltpu.CompilerParams(
            dimension_semantics=("arbitrary",), vmem_limit_bytes=VMEM_LIMIT),
        name="ffn_final" if final_norm else "ffn",
    )(x, g, w1, w3, w2, gf)


def _mixer_body(h_ref, g_ref, w_mix_ref, w_gate_ref, w_alow_ref, a_re_ref, a_im_ref, wb_ref, wc_ref,
                d_ref, glu_w_ref, glu_b_ref, a_up_w_ref, a_up_b_ref, gnorm_ref,
                p_s5_ref, p_gla_ref, w_out_ref, o_ref,
                s5_state, gla_state, bu_buf, loga_buf, bcum_buf, *, bsz, steps, chunk):
    rows = bsz * steps
    crows = bsz * chunk

    @pl.when(pl.program_id(0) == 0)
    def _():
        s5_state[...] = jnp.zeros_like(s5_state)
        gla_state[...] = jnp.zeros_like(gla_state)

    h = jnp.concatenate([h_ref[j] for j in range(SLABS)], axis=-1)
    u = (h * g_ref[...]).astype(BF16)
    inv = _inv_rms(h)

    def proj(col, width, w_ref=w_mix_ref):
        return _dot_nt(u, w_ref[col:col + width, :]) * inv

    s5_in = proj(C_S5, S5_WIDTH)
    s5_in_bf = s5_in.astype(BF16)
    y_parts = []
    for m in range(S5_BLOCKS):
        buf = bu_buf.at[m]
        buf[...] = _dot(s5_in_bf[:, m * LANES:(m + 1) * LANES], wb_ref[m])
        a_re = jnp.broadcast_to(a_re_ref[m], (bsz, S5_BLOCK_STATE))
        a_im = jnp.broadcast_to(a_im_ref[m], (bsz, S5_BLOCK_STATE))
        s_re = s5_state[m, :, :S5_BLOCK_STATE]
        s_im = s5_state[m, :, S5_BLOCK_STATE:]
        for t in range(steps):
            r0 = t * bsz
            n_re = a_re * s_re - a_im * s_im + buf[r0:r0 + bsz, :S5_BLOCK_STATE]
            n_im = a_re * s_im + a_im * s_re + buf[r0:r0 + bsz, S5_BLOCK_STATE:]
            buf[r0:r0 + bsz, :S5_BLOCK_STATE] = n_re
            buf[r0:r0 + bsz, S5_BLOCK_STATE:] = n_im
            s_re, s_im = n_re, n_im
        s5_state[m, :, :S5_BLOCK_STATE] = s_re
        s5_state[m, :, S5_BLOCK_STATE:] = s_im
        y_parts.append(_dot(buf[...].astype(BF16), wc_ref[m]))
    y = jnp.concatenate(y_parts, axis=-1) + d_ref[...] * s5_in
    z = jax.nn.gelu(y)
    y_s5 = z * jax.nn.sigmoid(_dot(z.astype(BF16), glu_w_ref[...]) + glu_b_ref[...])

    a_low = proj(0, A_LOW_PAD, w_alow_ref).astype(BF16)
    loga_buf[...] = jax.nn.log_sigmoid(_dot(a_low, a_up_w_ref[...]) + a_up_b_ref[...]) / GLA_TAU

    q = proj(C_Q, GLA_KEY_WIDTH) * (GLA_DK ** -0.5)
    k = proj(C_K, GLA_KEY_WIDTH)
    v = proj(C_V, GLA_VAL_WIDTH).astype(BF16)

    row = lax.broadcasted_iota(jnp.int32, (crows, crows), 0)
    col = lax.broadcasted_iota(jnp.int32, (crows, crows), 1)
    causal_t = ((row % bsz) == (col % bsz)) & (row <= col)
    own_batch = (col // GLA_DK) == (row % bsz)
    own_batch8 = own_batch[:bsz]
    key_lane = lax.broadcasted_iota(jnp.int32, (crows, GLA_KEY_WIDTH), 1) // GLA_DK

    def expand(xh):
        rep = xh + pltpu.roll(xh, 2 * GLA_DK, 1)
        rep = rep + pltpu.roll(rep, GLA_DK, 1)
        return jnp.concatenate([rep, rep], axis=-1)

    o_chunks = []
    for c in range(steps // chunk):
        c0 = c * crows
        b_last = jnp.zeros((bsz, GLA_KEY_WIDTH), F32)
        for t in range(chunk):
            r0 = c0 + t * bsz
            b_last = b_last + loga_buf[r0:r0 + bsz, :]
            bcum_buf[r0:r0 + bsz, :] = b_last
        bcum = bcum_buf[c0:c0 + crows, :]
        b_last_rows = jnp.broadcast_to(b_last[None], (chunk, bsz, GLA_KEY_WIDTH)).reshape(
            crows, GLA_KEY_WIDTH)
        k_c = k[c0:c0 + crows]
        q_t = q[c0:c0 + crows] * jnp.exp(bcum)
        k_t = (k_c * jnp.exp(-bcum)).astype(BF16)
        k_end = k_c * jnp.exp(b_last_rows - bcum)
        o_parts = []
        for hd in range(GLA_HEADS):
            head = key_lane == hd
            q_h = jnp.where(head, q_t, 0.0)
            p_t = jnp.where(causal_t, _dot_nt(k_t, q_h.astype(BF16)), 0.0).astype(BF16)
            v_h = v[c0:c0 + crows, hd * GLA_DV:(hd + 1) * GLA_DV]
            q_exp = jnp.where(own_batch, expand(q_h), 0.0).astype(BF16)
            s_prev = gla_state[hd]
            o_t = _dot_tn(v_h, p_t) + _dot_nt(s_prev.astype(BF16), q_exp)
            o_h = o_t.T
            k_exp = jnp.where(own_batch, expand(jnp.where(head, k_end, 0.0)), 0.0).astype(BF16)
            d_state = _dot_tn(v_h, k_exp)
            bl = expand(jnp.where(head[:bsz], b_last, 0.0))
            decay = jnp.exp(jnp.sum(jnp.where(own_batch8, bl, 0.0), axis=0, keepdims=True))
            gla_state[hd] = decay * s_prev + d_state
            o_parts.append(o_h * lax.rsqrt(jnp.mean(o_h * o_h, axis=-1, keepdims=True) + EPS))
        o_chunks.append(jnp.concatenate(o_parts, axis=-1))
    o = jnp.concatenate(o_chunks, axis=0) * gnorm_ref[...]
    r = proj(C_R, GLA_VAL_WIDTH)
    y_gla = o * (r * jax.nn.sigmoid(r))

    g_s5 = jax.nn.sigmoid(proj(C_GS5, D_MODEL, w_gate_ref))
    g_gla = jax.nn.sigmoid(proj(C_GGLA, D_MODEL, w_gate_ref))
    merged = (g_s5 * _dot(y_s5.astype(BF16), p_s5_ref[...])
              + g_gla * _dot(y_gla.astype(BF16), p_gla_ref[...]))
    out = h + _dot(merged.astype(BF16), w_out_ref[...])
    for j in range(SLABS):
        o_ref[j] = out[:, j * LANES:(j + 1) * LANES]


def _mixer_call(h, g, w_mix, w_gate, w_alow, a_re, a_im, wb, wc, d_skip, glu_w, glu_b, a_up_w,
                a_up_b, gnorm, p_s5, p_gla, w_out, *, bsz, seq):
    steps = min(MIX_STEPS, seq)
    rows = bsz * steps
    args = (g, w_mix, w_gate, w_alow, a_re, a_im, wb, wc, d_skip, glu_w, glu_b, a_up_w, a_up_b,
            gnorm, p_s5, p_gla, w_out)
    tile_spec = pl.BlockSpec((SLABS, rows, LANES), lambda i: (0, i, 0))
    return pl.pallas_call(
        functools.partial(_mixer_body, bsz=bsz, steps=steps, chunk=GLA_CHUNK),
        grid=(seq // steps,),
        in_specs=[tile_spec] + [_const_spec(a.shape) for a in args],
        out_specs=tile_spec,
        out_shape=jax.ShapeDtypeStruct((SLABS, seq * bsz, LANES), F32),
        scratch_shapes=[
            pltpu.VMEM((S5_BLOCKS, bsz, 2 * S5_BLOCK_STATE), F32),
            pltpu.VMEM((GLA_HEADS, GLA_DV, bsz * GLA_DK), F32),
            pltpu.VMEM((S5_BLOCKS, rows, 2 * S5_BLOCK_STATE), F32),
            pltpu.VMEM((rows, GLA_KEY_WIDTH), F32),
            pltpu.VMEM((rows, GLA_KEY_WIDTH), F32),
        ],
        compiler_params=pltpu.CompilerParams(
            dimension_semantics=("arbitrary",), vmem_limit_bytes=VMEM_LIMIT),
        name="mixer",
    )(h, *args)


def _s5_discretize_body(lam_re_ref, lam_im_ref, log_dt_ref, b_re_ref, b_im_ref,
                        ar_ref, ai_ref, bbar_re_ref, bbar_im_ref):
    lam_re = lam_re_ref[...]
    lam_im = lam_im_ref[...]
    dt = jnp.exp(log_dt_ref[...])
    mag = jnp.exp(lam_re * dt)
    ar = mag * jnp.cos(lam_im * dt)
    ai = mag * jnp.sin(lam_im * dt)
    den = lam_re * lam_re + lam_im * lam_im
    nr = ar - 1.0
    fr = ((nr * lam_re + ai * lam_im) / den)[:, None, :]
    fi = ((ai * lam_re - nr * lam_im) / den)[:, None, :]
    ar_ref[...] = ar
    ai_ref[...] = ai
    b_re = b_re_ref[...]
    b_im = b_im_ref[...]
    bbar_re_ref[...] = fr * b_re - fi * b_im
    bbar_im_ref[...] = fr * b_im + fi * b_re


def _s5_discretize(lam_re, lam_im, log_dt, b_re, b_im):
    gp = jax.ShapeDtypeStruct((S5_GROUPS, S5_STATE), F32)
    ghp = jax.ShapeDtypeStruct((S5_GROUPS, S5_GROUP, S5_STATE), F32)
    return pl.pallas_call(
        _s5_discretize_body, out_shape=(gp, gp, ghp, ghp), name="s5_discretize",
    )(lam_re, lam_im, log_dt.reshape(S5_GROUPS, 1),
      jnp.swapaxes(b_re, 1, 2), jnp.swapaxes(b_im, 1, 2))


def _pack_s5(ar, ai, bbar_re, bbar_im, c_re, c_im):
    eye = jnp.eye(S5_BLOCK_GROUPS, dtype=F32)
    nb, ng, ns, nh = S5_BLOCKS, S5_BLOCK_GROUPS, S5_STATE, S5_GROUP
    bb = jnp.stack([bbar_re, bbar_im]).reshape(2, nb, ng, nh, ns)
    wb = jnp.einsum('rmghp,gk->mghrkp', bb, eye).reshape(nb, ng * nh, 2 * ng * ns)
    cc = jnp.stack([c_re, -c_im]).reshape(2, nb, ng, nh, ns)
    wc = jnp.einsum('rmghp,gk->mrgpkh', cc, eye).reshape(nb, 2 * ng * ns, ng * nh)
    a_re = ar.reshape(nb, 1, ng * ns)
    a_im = ai.reshape(nb, 1, ng * ns)
    return a_re, a_im, wb.astype(BF16), wc.astype(BF16)


def kernel(x, ffn1_norm, ffn1_w1, ffn1_w3, ffn1_w2, mix_norm, w_in, s5_lambda_re, s5_lambda_im, s5_log_dt, s5_b_re, s5_b_im, s5_c_re, s5_c_im, s5_d, s5_glu_w, s5_glu_b, gla_a_up_w, gla_a_up_b, gla_out_norm, proj_s5, proj_gla, w_out, ffn2_norm, ffn2_w1, ffn2_w3, ffn2_w2, final_norm):
    bsz, seq, _ = x.shape
    assert bsz == SUBLANES and seq % GLA_CHUNK == 0 and ffn1_norm.shape[0] == 1
    assert seq % min(FFN_STEPS, seq) == 0 and seq % min(MIX_STEPS, seq) == 0
    assert MIX_STEPS % GLA_CHUNK == 0
    l = 0
    row = lambda a: a.reshape(1, -1).astype(F32)

    col = lambda a: a.reshape(-1, 1).astype(F32)
    h = _ffn_call(x, col(ffn1_norm[l]), ffn1_w1[l], ffn1_w3[l], ffn1_w2[l], row(final_norm),
                  batch_major_in=True, final_norm=False, bsz=bsz, seq=seq)

    assert IN_SIZES[5] == GLA_GATE_RANK and sum(IN_SIZES[:5]) == MIX_COLS
    w_in_t = jnp.swapaxes(w_in, 1, 2)[l].astype(BF16)
    gate_start = MIX_COLS + GLA_GATE_RANK
    w_mix = w_in_t[:MIX_COLS]
    w_gate = w_in_t[gate_start:gate_start + GATE_COLS]
    w_alow = jnp.pad(w_in_t[MIX_COLS:gate_start], ((0, A_LOW_PAD - GLA_GATE_RANK), (0, 0)))
    a_up_w = jnp.concatenate(
        [gla_a_up_w[l], jnp.zeros((A_LOW_PAD - GLA_GATE_RANK, GLA_KEY_WIDTH), F32)],
        axis=0).astype(BF16)
    ar, ai, bbar_re, bbar_im = _s5_discretize(
        s5_lambda_re[l], s5_lambda_im[l], s5_log_dt[l], s5_b_re[l], s5_b_im[l])
    a_re, a_im, wb, wc = _pack_s5(ar, ai, bbar_re, bbar_im, s5_c_re[l], s5_c_im[l])

    h = _mixer_call(h, row(mix_norm[l]), w_mix, w_gate, w_alow, a_re, a_im, wb, wc, row(s5_d[l]),
                    s5_glu_w[l].astype(BF16), row(s5_glu_b[l]), a_up_w, row(gla_a_up_b[l]),
                    row(gla_out_norm[l]), proj_s5[l].astype(BF16), proj_gla[l].astype(BF16),
                    w_out[l].astype(BF16), bsz=bsz, seq=seq)

    return _ffn_call(h, col(ffn2_norm[l]), ffn2_w1[l], ffn2_w3[l], ffn2_w2[l], row(final_norm),
                     batch_major_in=False, final_norm=True, bsz=bsz, seq=seq)
```

```python
import functools

import jax
import jax.numpy as jnp
from jax import lax
from jax.experimental import pallas as pl
from jax.experimental.pallas import tpu as pltpu

D_MODEL = 1024
S5_GROUP = 16
S5_WIDTH = 512
S5_GROUPS = 32
S5_STATE = 64
GLA_HEADS = 4
GLA_DV = 128
GLA_DK = 64
GLA_KEY_WIDTH = 256
GLA_VAL_WIDTH = 512
GLA_GATE_RANK = 16
GLA_TAU = 16.0
GLA_CHUNK = 64
D_FF = 2816
EPS = 1e-6
IN_SIZES = (S5_WIDTH, GLA_KEY_WIDTH, GLA_KEY_WIDTH, GLA_VAL_WIDTH, GLA_VAL_WIDTH,
            GLA_GATE_RANK, D_MODEL, D_MODEL)

LANES = 128
SUBLANES = 8
S5_BLOCKS = 4
S5_BLOCK_GROUPS = S5_GROUPS // S5_BLOCKS
S5_BLOCK_STATE = S5_BLOCK_GROUPS * S5_STATE
A_LOW_PAD = LANES
SLABS = D_MODEL // LANES
FFN_STEPS = 128
FFN_STREAMS = 2
MIX_STEPS = 128
FF_CHUNKS = (768, 768, 768, 512)
assert sum(FF_CHUNKS) == D_FF
FF_PREP_COLS = 256
FF_PREP_STEPS = D_FF // FF_PREP_COLS
assert FF_PREP_STEPS * FF_PREP_COLS == D_FF
VMEM_LIMIT = 56 * 1024 * 1024

BF16 = jnp.bfloat16
F32 = jnp.float32

C_S5 = 0
C_Q = C_S5 + S5_WIDTH
C_K = C_Q + GLA_KEY_WIDTH
C_V = C_K + GLA_KEY_WIDTH
C_R = C_V + GLA_VAL_WIDTH
MIX_COLS = C_R + GLA_VAL_WIDTH
C_GS5 = 0
C_GGLA = C_GS5 + D_MODEL
GATE_COLS = C_GGLA + D_MODEL


def _dot(a, b):
    return jnp.dot(a, b, preferred_element_type=F32)


def _dot_nt(a, b):
    return lax.dot_general(a, b, (((1,), (1,)), ((), ())), preferred_element_type=F32)


def _dot_tn(a, b):
    return lax.dot_general(a, b, (((0,), (0,)), ((), ())), preferred_element_type=F32)


def _inv_rms(x):
    return lax.rsqrt(jnp.mean(x * x, axis=-1, keepdims=True) + EPS)


def _rms_norm(x, g):
    return x * _inv_rms(x) * g


def _const_spec(shape):
    nd = len(shape)
    return pl.BlockSpec(shape, lambda *_: (0,) * nd, pipeline_mode=pl.Buffered(1))


def _ffn_body(x_ref, g_ref, w1_ref, w3_ref, w2_ref, gf_ref, o_ref, w1b, w3b, w2b, *,
              batch_major_in, final_norm, bsz, steps):
    step = pl.program_id(0)
    for c in range(FF_PREP_STEPS):
        @pl.when(step == c)
        def _(c=c):
            cols = slice(c * FF_PREP_COLS, (c + 1) * FF_PREP_COLS)
            gain = g_ref[...]
            w1b[:, cols] = (gain * w1_ref[...]).astype(BF16)
            w3b[:, cols] = (gain * w3_ref[...]).astype(BF16)
            w2b[cols, :] = w2_ref[...].astype(BF16)

    @pl.when(step >= FF_PREP_STEPS)
    def _():
        _ffn_tile(x_ref, w1b, w3b, w2b, gf_ref, o_ref, batch_major_in=batch_major_in,
                  final_norm=final_norm, bsz=bsz, steps=steps)


def _ffn_tile(x_ref, w1_ref, w3_ref, w2_ref, gf_ref, o_ref, *, bsz, **kwargs):
    per = bsz // FFN_STREAMS
    streams = [_ffn_stages(x_ref, w1_ref, w3_ref, w2_ref, gf_ref, o_ref,
                           b_lo=i * per, b_hi=(i + 1) * per, bsz=bsz, **kwargs)
               for i in range(FFN_STREAMS)]
    while streams:
        streams = [g for g in streams if next(g, "done") != "done"]


def _ffn_stages(x_ref, w1_ref, w3_ref, w2_ref, gf_ref, o_ref, *,
                b_lo, b_hi, batch_major_in, final_norm, bsz, steps):
    nb = b_hi - b_lo
    rows = nb * steps
    if batch_major_in:
        x = x_ref[b_lo:b_hi].reshape(rows, D_MODEL)
    else:
        x = jnp.concatenate(
            [jnp.concatenate([x_ref[j, pl.ds(bi, steps, stride=bsz), :] for j in range(SLABS)],
                             axis=-1) for bi in range(b_lo, b_hi)], axis=0)
    xb = x.astype(BF16)
    inv = _inv_rms(x)
    yield
    acc = None
    col = 0
    for width in FF_CHUNKS:
        a = _dot(xb, w1_ref[:, col:col + width]) * inv
        b = _dot(xb, w3_ref[:, col:col + width]) * inv
        hmid = (a * jax.nn.sigmoid(a) * b).astype(BF16)
        part = _dot(hmid, w2_ref[col:col + width, :])
        acc = part if acc is None else acc + part
        col += width
        yield
    y = x + 0.5 * acc
    if final_norm:
        y = _rms_norm(y, gf_ref[...])
    if batch_major_in:
        for bi in range(b_lo, b_hi):
            r0 = (bi - b_lo) * steps
            for j in range(SLABS):
                o_ref[j, pl.ds(bi, steps, stride=bsz), :] = (
                    y[r0:r0 + steps, j * LANES:(j + 1) * LANES])
    else:
        o_ref[b_lo:b_hi] = y.reshape(nb, steps, D_MODEL)


def _ffn_call(x, g, w1, w3, w2, gf, *, batch_major_in, final_norm, bsz, seq):
    steps = min(FFN_STEPS, seq)
    bm_block, tm_block = (bsz, steps, D_MODEL), (SLABS, steps * bsz, LANES)
    in_block, out_block = (bm_block, tm_block) if batch_major_in else (tm_block, bm_block)
    out_shape = (SLABS, seq * bsz, LANES) if batch_major_in else (bsz, seq, D_MODEL)
    tile = lambda i: (0, jnp.maximum(i - FF_PREP_STEPS, 0), 0)
    prep = lambda i: jnp.minimum(i, FF_PREP_STEPS - 1)
    return pl.pallas_call(
        functools.partial(_ffn_body, batch_major_in=batch_major_in, final_norm=final_norm,
                          bsz=bsz, steps=steps),
        grid=(FF_PREP_STEPS + seq // steps,),
        in_specs=[pl.BlockSpec(in_block, tile),
                  _const_spec((D_MODEL, 1)),
                  pl.BlockSpec((D_MODEL, FF_PREP_COLS), lambda i: (0, prep(i))),
                  pl.BlockSpec((D_MODEL, FF_PREP_COLS), lambda i: (0, prep(i))),
                  pl.BlockSpec((FF_PREP_COLS, D_MODEL), lambda i: (prep(i), 0)),
                  _const_spec((1, D_MODEL))],
        out_specs=pl.BlockSpec(out_block, tile),
        out_shape=jax.ShapeDtypeStruct(out_shape, F32),
        scratch_shapes=[pltpu.VMEM((D_MODEL, D_FF), BF16), pltpu.VMEM((D_MODEL, D_FF), BF16),
                        pltpu.VMEM((D_FF, D_MODEL), BF16)],
        compiler_params=pltpu.CompilerParams(
            dimension_semantics=("arbitrary",), vmem_limit_bytes=VMEM_LIMIT),
        name="ffn_final" if final_norm else "ffn",
    )(x, g, w1, w3, w2, gf)


def _mixer_body(h_ref, g_ref, w_mix_ref, w_gate_ref, w_alow_ref, a_re_ref, a_im_ref, wb_ref, wc_ref,
                d_ref, glu_w_ref, glu_b_ref, a_up_w_ref, a_up_b_ref, gnorm_ref,
                p_s5_ref, p_gla_ref, w_out_ref, o_ref,
                s5_state, gla_state, bu_buf, loga_buf, bcum_buf, *, bsz, steps, chunk):
    rows = bsz * steps
    crows = bsz * chunk

    @pl.when(pl.program_id(0) == 0)
    def _():
        s5_state[...] = jnp.zeros_like(s5_state)
        gla_state[...] = jnp.zeros_like(gla_state)

    h = jnp.concatenate([h_ref[j] for j in range(SLABS)], axis=-1)
    u = (h * g_ref[...]).astype(BF16)
    inv = _inv_rms(h)

    def proj(col, width, w_ref=w_mix_ref):
        return _dot_nt(u, w_ref[col:col + width, :]) * inv

    s5_in = proj(C_S5, S5_WIDTH)
    s5_in_bf = s5_in.astype(BF16)
    y_parts = []
    for m in range(S5_BLOCKS):
        buf = bu_buf.at[m]
        buf[...] = _dot(s5_in_bf[:, m * LANES:(m + 1) * LANES], wb_ref[m])
        a_re = jnp.broadcast_to(a_re_ref[m], (bsz, S5_BLOCK_STATE))
        a_im = jnp.broadcast_to(a_im_ref[m], (bsz, S5_BLOCK_STATE))
        s_re = s5_state[m, :, :S5_BLOCK_STATE]
        s_im = s5_state[m, :, S5_BLOCK_STATE:]
        for t in range(steps):
            r0 = t * bsz
            n_re = a_re * s_re - a_im * s_im + buf[r0:r0 + bsz, :S5_BLOCK_STATE]
            n_im = a_re * s_im + a_im * s_re + buf[r0:r0 + bsz, S5_BLOCK_STATE:]
            buf[r0:r0 + bsz, :S5_BLOCK_STATE] = n_re
            buf[r0:r0 + bsz, S5_BLOCK_STATE:] = n_im
            s_re, s_im = n_re, n_im
        s5_state[m, :, :S5_BLOCK_STATE] = s_re
        s5_state[m, :, S5_BLOCK_STATE:] = s_im
        y_parts.append(_dot(buf[...].astype(BF16), wc_ref[m]))
    y = jnp.concatenate(y_parts, axis=-1) + d_ref[...] * s5_in
    z = jax.nn.gelu(y)
    y_s5 = z * jax.nn.sigmoid(_dot(z.astype(BF16), glu_w_ref[...]) + glu_b_ref[...])

    a_low = proj(0, A_LOW_PAD, w_alow_ref).astype(BF16)
    loga_buf[...] = jax.nn.log_sigmoid(_dot(a_low, a_up_w_ref[...]) + a_up_b_ref[...]) / GLA_TAU

    q = proj(C_Q, GLA_KEY_WIDTH) * (GLA_DK ** -0.5)
    k = proj(C_K, GLA_KEY_WIDTH)
    v = proj(C_V, GLA_VAL_WIDTH).astype(BF16)

    row = lax.broadcasted_iota(jnp.int32, (crows, crows), 0)
    col = lax.broadcasted_iota(jnp.int32, (crows, crows), 1)
    causal = ((row % bsz) == (col % bsz)) & (col <= row)
    own_batch = (col // GLA_DK) == (row % bsz)
    own_batch8 = own_batch[:bsz]
    key_lane = lax.broadcasted_iota(jnp.int32, (crows, GLA_KEY_WIDTH), 1) // GLA_DK

    def expand(xh):
        rep = xh + pltpu.roll(xh, 2 * GLA_DK, 1)
        rep = rep + pltpu.roll(rep, GLA_DK, 1)
        return jnp.concatenate([rep, rep], axis=-1)

    o_chunks = []
    for c in range(steps // chunk):
        c0 = c * crows
        b_last = jnp.zeros((bsz, GLA_KEY_WIDTH), F32)
        for t in range(chunk):
            r0 = c0 + t * bsz
            b_last = b_last + loga_buf[r0:r0 + bsz, :]
            bcum_buf[r0:r0 + bsz, :] = b_last
        bcum = bcum_buf[c0:c0 + crows, :]
        b_last_rows = jnp.broadcast_to(b_last[None], (chunk, bsz, GLA_KEY_WIDTH)).reshape(
            crows, GLA_KEY_WIDTH)
        k_c = k[c0:c0 + crows]
        q_t = q[c0:c0 + crows] * jnp.exp(bcum)
        k_t = (k_c * jnp.exp(-bcum)).astype(BF16)
        k_end = k_c * jnp.exp(b_last_rows - bcum)
        o_parts = []
        for hd in range(GLA_HEADS):
            head = key_lane == hd
            q_h = jnp.where(head, q_t, 0.0)
            scores = _dot_nt(q_h.astype(BF16), k_t)
            p = jnp.where(causal, scores, 0.0).astype(BF16)
            v_h = v[c0:c0 + crows, hd * GLA_DV:(hd + 1) * GLA_DV]
            o_h = _dot(p, v_h)
            q_exp = jnp.where(own_batch, expand(q_h), 0.0).astype(BF16)
            s_prev = gla_state[hd]
            o_h = o_h + _dot_nt(q_exp, s_prev.astype(BF16))
            k_exp = jnp.where(own_batch, expand(jnp.where(head, k_end, 0.0)), 0.0).astype(BF16)
            d_state = _dot_tn(v_h, k_exp)
            bl = expand(jnp.where(head[:bsz], b_last, 0.0))
            decay = jnp.exp(jnp.sum(jnp.where(own_batch8, bl, 0.0), axis=0, keepdims=True))
            gla_state[hd] = decay * s_prev + d_state
            o_parts.append(o_h * lax.rsqrt(jnp.mean(o_h * o_h, axis=-1, keepdims=True) + EPS))
        o_chunks.append(jnp.concatenate(o_parts, axis=-1))
    o = jnp.concatenate(o_chunks, axis=0) * gnorm_ref[...]
    r = proj(C_R, GLA_VAL_WIDTH)
    y_gla = o * (r * jax.nn.sigmoid(r))

    g_s5 = jax.nn.sigmoid(proj(C_GS5, D_MODEL, w_gate_ref))
    g_gla = jax.nn.sigmoid(proj(C_GGLA, D_MODEL, w_gate_ref))
    merged = (g_s5 * _dot(y_s5.astype(BF16), p_s5_ref[...])
              + g_gla * _dot(y_gla.astype(BF16), p_gla_ref[...]))
    out = h + _dot(merged.astype(BF16), w_out_ref[...])
    for j in range(SLABS):
        o_ref[j] = out[:, j * LANES:(j + 1) * LANES]


def _mixer_call(h, g, w_mix, w_gate, w_alow, a_re, a_im, wb, wc, d_skip, glu_w, glu_b, a_up_w,
                a_up_b, gnorm, p_s5, p_gla, w_out, *, bsz, seq):
    steps = min(MIX_STEPS, seq)
    rows = bsz * steps
    args = (g, w_mix, w_gate, w_alow, a_re, a_im, wb, wc, d_skip, glu_w, glu_b, a_up_w, a_up_b,
            gnorm, p_s5, p_gla, w_out)
    tile_spec = pl.BlockSpec((SLABS, rows, LANES), lambda i: (0, i, 0))
    return pl.pallas_call(
        functools.partial(_mixer_body, bsz=bsz, steps=steps, chunk=GLA_CHUNK),
        grid=(seq // steps,),
        in_specs=[tile_spec] + [_const_spec(a.shape) for a in args],
        out_specs=tile_spec,
        out_shape=jax.ShapeDtypeStruct((SLABS, seq * bsz, LANES), F32),
        scratch_shapes=[
            pltpu.VMEM((S5_BLOCKS, bsz, 2 * S5_BLOCK_STATE), F32),
            pltpu.VMEM((GLA_HEADS, GLA_DV, bsz * GLA_DK), F32),
            pltpu.VMEM((S5_BLOCKS, rows, 2 * S5_BLOCK_STATE), F32),
            pltpu.VMEM((rows, GLA_KEY_WIDTH), F32),
            pltpu.VMEM((rows, GLA_KEY_WIDTH), F32),
        ],
        compiler_params=pltpu.CompilerParams(
            dimension_semantics=("arbitrary",), vmem_limit_bytes=VMEM_LIMIT),
        name="mixer",
    )(h, *args)


def _s5_discretize_body(lam_re_ref, lam_im_ref, log_dt_ref, b_re_ref, b_im_ref,
                        ar_ref, ai_ref, bbar_re_ref, bbar_im_ref):
    lam_re = lam_re_ref[...]
    lam_im = lam_im_ref[...]
    dt = jnp.exp(log_dt_ref[...])
    mag = jnp.exp(lam_re * dt)
    ar = mag * jnp.cos(lam_im * dt)
    ai = mag * jnp.sin(lam_im * dt)
    den = lam_re * lam_re + lam_im * lam_im
    nr = ar - 1.0
    fr = ((nr * lam_re + ai * lam_im) / den)[:, None, :]
    fi = ((ai * lam_re - nr * lam_im) / den)[:, None, :]
    ar_ref[...] = ar
    ai_ref[...] = ai
    b_re = b_re_ref[...]
    b_im = b_im_ref[...]
    bbar_re_ref[...] = fr * b_re - fi * b_im
    bbar_im_ref[...] = fr * b_im + fi * b_re


def _s5_discretize(lam_re, lam_im, log_dt, b_re, b_im):
    gp = jax.ShapeDtypeStruct((S5_GROUPS, S5_STATE), F32)
    ghp = jax.ShapeDtypeStruct((S5_GROUPS, S5_GROUP, S5_STATE), F32)
    return pl.pallas_call(
        _s5_discretize_body, out_shape=(gp, gp, ghp, ghp), name="s5_discretize",
    )(lam_re, lam_im, log_dt.reshape(S5_GROUPS, 1),
      jnp.swapaxes(b_re, 1, 2), jnp.swapaxes(b_im, 1, 2))


def _pack_s5(ar, ai, bbar_re, bbar_im, c_re, c_im):
    eye = jnp.eye(S5_BLOCK_GROUPS, dtype=F32)
    nb, ng, ns, nh = S5_BLOCKS, S5_BLOCK_GROUPS, S5_STATE, S5_GROUP
    bb = jnp.stack([bbar_re, bbar_im]).reshape(2, nb, ng, nh, ns)
    wb = jnp.einsum('rmghp,gk->mghrkp', bb, eye).reshape(nb, ng * nh, 2 * ng * ns)
    cc = jnp.stack([c_re, -c_im]).reshape(2, nb, ng, nh, ns)
    wc = jnp.einsum('rmghp,gk->mrgpkh', cc, eye).reshape(nb, 2 * ng * ns, ng * nh)
    a_re = ar.reshape(nb, 1, ng * ns)
    a_im = ai.reshape(nb, 1, ng * ns)
    return a_re, a_im, wb.astype(BF16), wc.astype(BF16)


def kernel(x, ffn1_norm, ffn1_w1, ffn1_w3, ffn1_w2, mix_norm, w_in, s5_lambda_re, s5_lambda_im, s5_log_dt, s5_b_re, s5_b_im, s5_c_re, s5_c_im, s5_d, s5_glu_w, s5_glu_b, gla_a_up_w, gla_a_up_b, gla_out_norm, proj_s5, proj_gla, w_out, ffn2_norm, ffn2_w1, ffn2_w3, ffn2_w2, final_norm):
    bsz, seq, _ = x.shape
    assert bsz == SUBLANES and seq % GLA_CHUNK == 0 and ffn1_norm.shape[0] == 1
    assert seq % min(FFN_STEPS, seq) == 0 and seq % min(MIX_STEPS, seq) == 0
    assert MIX_STEPS % GLA_CHUNK == 0
    l = 0
    row = lambda a: a.reshape(1, -1).astype(F32)

    col = lambda a: a.reshape(-1, 1).astype(F32)
    h = _ffn_call(x, col(ffn1_norm[l]), ffn1_w1[l], ffn1_w3[l], ffn1_w2[l], row(final_norm),
                  batch_major_in=True, final_norm=False, bsz=bsz, seq=seq)

    assert IN_SIZES[5] == GLA_GATE_RANK and sum(IN_SIZES[:5]) == MIX_COLS
    w_in_t = jnp.swapaxes(w_in, 1, 2)[l].astype(BF16)
    gate_start = MIX_COLS + GLA_GATE_RANK
    w_mix = w_in_t[:MIX_COLS]
    w_gate = w_in_t[gate_start:gate_start + GATE_COLS]
    w_alow = jnp.pad(w_in_t[MIX_COLS:gate_start], ((0, A_LOW_PAD - GLA_GATE_RANK), (0, 0)))
    a_up_w = jnp.concatenate(
        [gla_a_up_w[l], jnp.zeros((A_LOW_PAD - GLA_GATE_RANK, GLA_KEY_WIDTH), F32)],
        axis=0).astype(BF16)
    ar, ai, bbar_re, bbar_im = _s5_discretize(
        s5_lambda_re[l], s5_lambda_im[l], s5_log_dt[l], s5_b_re[l], s5_b_im[l])
    a_re, a_im, wb, wc = _pack_s5(ar, ai, bbar_re, bbar_im, s5_c_re[l], s5_c_im[l])

    h = _mixer_call(h, row(mix_norm[l]), w_mix, w_gate, w_alow, a_re, a_im, wb, wc, row(s5_d[l]),
                    s5_glu_w[l].astype(BF16), row(s5_glu_b[l]), a_up_w, row(gla_a_up_b[l]),
                    row(gla_out_norm[l]), proj_s5[l].astype(BF16), proj_gla[l].astype(BF16),
                    w_out[l].astype(BF16), bsz=bsz, seq=seq)

    return _ffn_call(h, col(ffn2_norm[l]), ffn2_w1[l], ffn2_w3[l], ffn2_w2[l], row(final_norm),
                     batch_major_in=False, final_norm=True, bsz=bsz, seq=seq)
```

```python
import functools

import jax
import jax.numpy as jnp
from jax import lax
from jax.experimental import pallas as pl
from jax.experimental.pallas import tpu as pltpu

D_MODEL = 1024
S5_GROUP = 16
S5_WIDTH = 512
S5_GROUPS = 32
S5_STATE = 64
GLA_HEADS = 4
GLA_DV = 128
GLA_DK = 64
GLA_KEY_WIDTH = 256
GLA_VAL_WIDTH = 512
GLA_GATE_RANK = 16
GLA_TAU = 16.0
GLA_CHUNK = 64
D_FF = 2816
EPS = 1e-6
IN_SIZES = (S5_WIDTH, GLA_KEY_WIDTH, GLA_KEY_WIDTH, GLA_VAL_WIDTH, GLA_VAL_WIDTH,
            GLA_GATE_RANK, D_MODEL, D_MODEL)

LANES = 128
SUBLANES = 8
S5_BLOCKS = 4
S5_BLOCK_GROUPS = S5_GROUPS // S5_BLOCKS
S5_BLOCK_STATE = S5_BLOCK_GROUPS * S5_STATE
A_LOW_PAD = LANES
SLABS = D_MODEL // LANES
FFN_STEPS = 128
MIX_STEPS = 128
FF_CHUNKS = (768, 768, 768, 512)
assert sum(FF_CHUNKS) == D_FF
FF_PREP_COLS = 256
FF_PREP_STEPS = D_FF // FF_PREP_COLS
assert FF_PREP_STEPS * FF_PREP_COLS == D_FF
VMEM_LIMIT = 56 * 1024 * 1024

BF16 = jnp.bfloat16
F32 = jnp.float32

C_S5 = 0
C_Q = C_S5 + S5_WIDTH
C_K = C_Q + GLA_KEY_WIDTH
C_V = C_K + GLA_KEY_WIDTH
C_R = C_V + GLA_VAL_WIDTH
MIX_COLS = C_R + GLA_VAL_WIDTH
C_GS5 = 0
C_GGLA = C_GS5 + D_MODEL
GATE_COLS = C_GGLA + D_MODEL


def _dot(a, b):
    return jnp.dot(a, b, preferred_element_type=F32)


def _dot_nt(a, b):
    return lax.dot_general(a, b, (((1,), (1,)), ((), ())), preferred_element_type=F32)


def _dot_tn(a, b):
    return lax.dot_general(a, b, (((0,), (0,)), ((), ())), preferred_element_type=F32)


def _inv_rms(x):
    return lax.rsqrt(jnp.mean(x * x, axis=-1, keepdims=True) + EPS)


def _rms_norm(x, g):
    return x * _inv_rms(x) * g


def _const_spec(shape):
    nd = len(shape)
    return pl.BlockSpec(shape, lambda *_: (0,) * nd, pipeline_mode=pl.Buffered(1))


def _ffn_body(x_ref, g_ref, w1_ref, w3_ref, w2_ref, gf_ref, o_ref, w1b, w3b, w2b, *,
              batch_major_in, final_norm, bsz, steps):
    step = pl.program_id(0)
    for c in range(FF_PREP_STEPS):
        @pl.when(step == c)
        def _(c=c):
            cols = slice(c * FF_PREP_COLS, (c + 1) * FF_PREP_COLS)
            gain = g_ref[...]
            w1b[:, cols] = (gain * w1_ref[...]).astype(BF16)
            w3b[:, cols] = (gain * w3_ref[...]).astype(BF16)
            w2b[cols, :] = (0.5 * w2_ref[...]).astype(BF16)

    @pl.when(step >= FF_PREP_STEPS)
    def _():
        _ffn_tile(x_ref, w1b, w3b, w2b, gf_ref, o_ref, batch_major_in=batch_major_in,
                  final_norm=final_norm, bsz=bsz, steps=steps)


def _ffn_tile(x_ref, w1_ref, w3_ref, w2_ref, gf_ref, o_ref, *,
              batch_major_in, final_norm, bsz, steps):
    rows = bsz * steps
    if batch_major_in:
        x = x_ref[...].reshape(rows, D_MODEL)
    else:
        x = jnp.concatenate(
            [jnp.concatenate([x_ref[j, pl.ds(bi, steps, stride=bsz), :] for j in range(SLABS)],
                             axis=-1) for bi in range(bsz)], axis=0)
    xb = x.astype(BF16)
    inv = _inv_rms(x)
    acc = None
    col = 0
    for width in FF_CHUNKS:
        a = _dot(xb, w1_ref[:, col:col + width]) * inv
        b = _dot(xb, w3_ref[:, col:col + width]) * inv
        hmid = (a * jax.nn.sigmoid(a) * b).astype(BF16)
        part = _dot(hmid, w2_ref[col:col + width, :])
        acc = part if acc is None else acc + part
        col += width
    y = x + acc
    if final_norm:
        y = _rms_norm(y, gf_ref[...])
    if batch_major_in:
        for bi in range(bsz):
            for j in range(SLABS):
                o_ref[j, pl.ds(bi, steps, stride=bsz), :] = (
                    y[bi * steps:(bi + 1) * steps, j * LANES:(j + 1) * LANES])
    else:
        o_ref[...] = y.reshape(bsz, steps, D_MODEL)


def _ffn_call(x, g, w1, w3, w2, gf, *, batch_major_in, final_norm, bsz, seq):
    steps = min(FFN_STEPS, seq)
    bm_block, tm_block = (bsz, steps, D_MODEL), (SLABS, steps * bsz, LANES)
    in_block, out_block = (bm_block, tm_block) if batch_major_in else (tm_block, bm_block)
    out_shape = (SLABS, seq * bsz, LANES) if batch_major_in else (bsz, seq, D_MODEL)
    tile = lambda i: (0, jnp.maximum(i - FF_PREP_STEPS, 0), 0)
    prep = lambda i: jnp.minimum(i, FF_PREP_STEPS - 1)
    return pl.pallas_call(
        functools.partial(_ffn_body, batch_major_in=batch_major_in, final_norm=final_norm,
                          bsz=bsz, steps=steps),
        grid=(FF_PREP_STEPS + seq // steps,),
        in_specs=[pl.BlockSpec(in_block, tile),
                  _const_spec((D_MODEL, 1)),
                  pl.BlockSpec((D_MODEL, FF_PREP_COLS), lambda i: (0, prep(i))),
                  pl.BlockSpec((D_MODEL, FF_PREP_COLS), lambda i: (0, prep(i))),
                  pl.BlockSpec((FF_PREP_COLS, D_MODEL), lambda i: (prep(i), 0)),
                  _const_spec((1, D_MODEL))],
        out_specs=pl.BlockSpec(out_block, tile),
        out_shape=jax.ShapeDtypeStruct(out_shape, F32),
        scratch_shapes=[pltpu.VMEM((D_MODEL, D_FF), BF16), pltpu.VMEM((D_MODEL, D_FF), BF16),
                        pltpu.VMEM((D_FF, D_MODEL), BF16)],
        compiler_params=pltpu.CompilerParams(
            dimension_semantics=("arbitrary",), vmem_limit_bytes=VMEM_LIMIT),
        name="ffn_final" if final_norm else "ffn",
    )(x, g, w1, w3, w2, gf)


def _mixer_body(h_ref, g_ref, w_mix_ref, w_gate_ref, w_alow_ref, a_re_ref, a_im_ref, wb_ref, wc_ref,
                d_ref, glu_w_ref, glu_b_ref, a_up_w_ref, a_up_b_ref, gnorm_ref,
                p_s5_ref, p_gla_ref, w_out_ref, o_ref,
                s5_state, gla_state, bu_buf, loga_buf, bcum_buf, *, bsz, steps, chunk):
    rows = bsz * steps
    crows = bsz * chunk

    @pl.when(pl.program_id(0) == 0)
    def _():
        s5_state[...] = jnp.zeros_like(s5_state)
        gla_state[...] = jnp.zeros_like(gla_state)

    h = jnp.concatenate([h_ref[j] for j in range(SLABS)], axis=-1)
    u = (h * g_ref[...]).astype(BF16)
    inv = _inv_rms(h)

    def proj(col, width, w_ref=w_mix_ref):
        return _dot_nt(u, w_ref[col:col + width, :]) * inv

    s5_in = proj(C_S5, S5_WIDTH)
    s5_in_bf = s5_in.astype(BF16)
    y_parts = []
    for m in range(S5_BLOCKS):
        buf = bu_buf.at[m]
        buf[...] = _dot(s5_in_bf[:, m * LANES:(m + 1) * LANES], wb_ref[m])
        a_re = jnp.broadcast_to(a_re_ref[m], (bsz, S5_BLOCK_STATE))
        a_im = jnp.broadcast_to(a_im_ref[m], (bsz, S5_BLOCK_STATE))
        s_re = s5_state[m, :, :S5_BLOCK_STATE]
        s_im = s5_state[m, :, S5_BLOCK_STATE:]
        for t in range(steps):
            r0 = t * bsz
            n_re = a_re * s_re - a_im * s_im + buf[r0:r0 + bsz, :S5_BLOCK_STATE]
            n_im = a_re * s_im + a_im * s_re + buf[r0:r0 + bsz, S5_BLOCK_STATE:]
            buf[r0:r0 + bsz, :S5_BLOCK_STATE] = n_re
            buf[r0:r0 + bsz, S5_BLOCK_STATE:] = n_im
            s_re, s_im = n_re, n_im
        s5_state[m, :, :S5_BLOCK_STATE] = s_re
        s5_state[m, :, S5_BLOCK_STATE:] = s_im
        y_parts.append(_dot(buf[...].astype(BF16), wc_ref[m]))
    y = jnp.concatenate(y_parts, axis=-1) + d_ref[...] * s5_in
    z = jax.nn.gelu(y)
    y_s5 = z * jax.nn.sigmoid(_dot(z.astype(BF16), glu_w_ref[...]) + glu_b_ref[...])

    a_low = proj(0, A_LOW_PAD, w_alow_ref).astype(BF16)
    loga_buf[...] = jax.nn.log_sigmoid(_dot(a_low, a_up_w_ref[...]) + a_up_b_ref[...]) / GLA_TAU

    q = proj(C_Q, GLA_KEY_WIDTH) * (GLA_DK ** -0.5)
    k = proj(C_K, GLA_KEY_WIDTH)
    v = proj(C_V, GLA_VAL_WIDTH).astype(BF16)

    row = lax.broadcasted_iota(jnp.int32, (crows, crows), 0)
    col = lax.broadcasted_iota(jnp.int32, (crows, crows), 1)
    causal = ((row % bsz) == (col % bsz)) & (col <= row)
    own_batch = (col // GLA_DK) == (row % bsz)
    own_batch8 = own_batch[:bsz]
    key_lane = lax.broadcasted_iota(jnp.int32, (crows, GLA_KEY_WIDTH), 1) // GLA_DK

    def expand(xh):
        rep = xh + pltpu.roll(xh, 2 * GLA_DK, 1)
        rep = rep + pltpu.roll(rep, GLA_DK, 1)
        return jnp.concatenate([rep, rep], axis=-1)

    o_chunks = []
    for c in range(steps // chunk):
        c0 = c * crows
        b_last = jnp.zeros((bsz, GLA_KEY_WIDTH), F32)
        for t in range(chunk):
            r0 = c0 + t * bsz
            b_last = b_last + loga_buf[r0:r0 + bsz, :]
            bcum_buf[r0:r0 + bsz, :] = b_last
        bcum = bcum_buf[c0:c0 + crows, :]
        b_last_rows = jnp.broadcast_to(b_last[None], (chunk, bsz, GLA_KEY_WIDTH)).reshape(
            crows, GLA_KEY_WIDTH)
        k_c = k[c0:c0 + crows]
        q_t = q[c0:c0 + crows] * jnp.exp(bcum)
        k_t = (k_c * jnp.exp(-bcum)).astype(BF16)
        k_end = k_c * jnp.exp(b_last_rows - bcum)
        o_parts = []
        for hd in range(GLA_HEADS):
            head = key_lane == hd
            q_h = jnp.where(head, q_t, 0.0)
            scores = _dot_nt(q_h.astype(BF16), k_t)
            p = jnp.where(causal, scores, 0.0).astype(BF16)
            v_h = v[c0:c0 + crows, hd * GLA_DV:(hd + 1) * GLA_DV]
            q_exp = jnp.where(own_batch, expand(q_h), 0.0).astype(BF16)
            s_prev = gla_state[hd]
            o_h = _dot(jnp.concatenate([p, q_exp], axis=1),
                       jnp.concatenate([v_h, s_prev.T.astype(BF16)], axis=0))
            k_exp = jnp.where(own_batch, expand(jnp.where(head, k_end, 0.0)), 0.0).astype(BF16)
            d_state = _dot_tn(v_h, k_exp)
            bl = expand(jnp.where(head[:bsz], b_last, 0.0))
            decay = jnp.exp(jnp.sum(jnp.where(own_batch8, bl, 0.0), axis=0, keepdims=True))
            gla_state[hd] = decay * s_prev + d_state
            o_parts.append(o_h * lax.rsqrt(jnp.mean(o_h * o_h, axis=-1, keepdims=True) + EPS))
        o_chunks.append(jnp.concatenate(o_parts, axis=-1))
    o = jnp.concatenate(o_chunks, axis=0) * gnorm_ref[...]
    r = proj(C_R, GLA_VAL_WIDTH)
    y_gla = o * (r * jax.nn.sigmoid(r))

    g_s5 = jax.nn.sigmoid(proj(C_GS5, D_MODEL, w_gate_ref))
    g_gla = jax.nn.sigmoid(proj(C_GGLA, D_MODEL, w_gate_ref))
    merged = (g_s5 * _dot(y_s5.astype(BF16), p_s5_ref[...])
              + g_gla * _dot(y_gla.astype(BF16), p_gla_ref[...]))
    out = h + _dot(merged.astype(BF16), w_out_ref[...])
    for j in range(SLABS):
        o_ref[j] = out[:, j * LANES:(j + 1) * LANES]


def _mixer_call(h, g, w_mix, w_gate, w_alow, a_re, a_im, wb, wc, d_skip, glu_w, glu_b, a_up_w,
                a_up_b, gnorm, p_s5, p_gla, w_out, *, bsz, seq):
    steps = min(MIX_STEPS, seq)
    rows = bsz * steps
    args = (g, w_mix, w_gate, w_alow, a_re, a_im, wb, wc, d_skip, glu_w, glu_b, a_up_w, a_up_b,
            gnorm, p_s5, p_gla, w_out)
    tile_spec = pl.BlockSpec((SLABS, rows, LANES), lambda i: (0, i, 0))
    return pl.pallas_call(
        functools.partial(_mixer_body, bsz=bsz, steps=steps, chunk=GLA_CHUNK),
        grid=(seq // steps,),
        in_specs=[tile_spec] + [_const_spec(a.shape) for a in args],
        out_specs=tile_spec,
        out_shape=jax.ShapeDtypeStruct((SLABS, seq * bsz, LANES), F32),
        scratch_shapes=[
            pltpu.VMEM((S5_BLOCKS, bsz, 2 * S5_BLOCK_STATE), F32),
            pltpu.VMEM((GLA_HEADS, GLA_DV, bsz * GLA_DK), F32),
            pltpu.VMEM((S5_BLOCKS, rows, 2 * S5_BLOCK_STATE), F32),
            pltpu.VMEM((rows, GLA_KEY_WIDTH), F32),
            pltpu.VMEM((rows, GLA_KEY_WIDTH), F32),
        ],
        compiler_params=pltpu.CompilerParams(
            dimension_semantics=("arbitrary",), vmem_limit_bytes=VMEM_LIMIT),
        name="mixer",
    )(h, *args)


def _s5_discretize_body(lam_re_ref, lam_im_ref, log_dt_ref, b_re_ref, b_im_ref,
                        ar_ref, ai_ref, bbar_re_ref, bbar_im_ref):
    lam_re = lam_re_ref[...]
    lam_im = lam_im_ref[...]
    dt = jnp.exp(log_dt_ref[...])
    mag = jnp.exp(lam_re * dt)
    ar = mag * jnp.cos(lam_im * dt)
    ai = mag * jnp.sin(lam_im * dt)
    den = lam_re * lam_re + lam_im * lam_im
    nr = ar - 1.0
    fr = ((nr * lam_re + ai * lam_im) / den)[:, None, :]
    fi = ((ai * lam_re - nr * lam_im) / den)[:, None, :]
    ar_ref[...] = ar
    ai_ref[...] = ai
    b_re = b_re_ref[...]
    b_im = b_im_ref[...]
    bbar_re_ref[...] = fr * b_re - fi * b_im
    bbar_im_ref[...] = fr * b_im + fi * b_re


def _s5_discretize(lam_re, lam_im, log_dt, b_re, b_im):
    gp = jax.ShapeDtypeStruct((S5_GROUPS, S5_STATE), F32)
    ghp = jax.ShapeDtypeStruct((S5_GROUPS, S5_GROUP, S5_STATE), F32)
    return pl.pallas_call(
        _s5_discretize_body, out_shape=(gp, gp, ghp, ghp), name="s5_discretize",
    )(lam_re, lam_im, log_dt.reshape(S5_GROUPS, 1),
      jnp.swapaxes(b_re, 1, 2), jnp.swapaxes(b_im, 1, 2))


def _pack_s5(ar, ai, bbar_re, bbar_im, c_re, c_im):
    eye = jnp.eye(S5_BLOCK_GROUPS, dtype=F32)
    nb, ng, ns, nh = S5_BLOCKS, S5_BLOCK_GROUPS, S5_STATE, S5_GROUP
    bb = jnp.stack([bbar_re, bbar_im]).reshape(2, nb, ng, nh, ns)
    wb = jnp.einsum('rmghp,gk->mghrkp', bb, eye).reshape(nb, ng * nh, 2 * ng * ns)
    cc = jnp.stack([c_re, -c_im]).reshape(2, nb, ng, nh, ns)
    wc = jnp.einsum('rmghp,gk->mrgpkh', cc, eye).reshape(nb, 2 * ng * ns, ng * nh)
    a_re = ar.reshape(nb, 1, ng * ns)
    a_im = ai.reshape(nb, 1, ng * ns)
    return a_re, a_im, wb.astype(BF16), wc.astype(BF16)


def kernel(x, ffn1_norm, ffn1_w1, ffn1_w3, ffn1_w2, mix_norm, w_in, s5_lambda_re, s5_lambda_im, s5_log_dt, s5_b_re, s5_b_im, s5_c_re, s5_c_im, s5_d, s5_glu_w, s5_glu_b, gla_a_up_w, gla_a_up_b, gla_out_norm, proj_s5, proj_gla, w_out, ffn2_norm, ffn2_w1, ffn2_w3, ffn2_w2, final_norm):
    bsz, seq, _ = x.shape
    assert bsz == SUBLANES and seq % GLA_CHUNK == 0 and ffn1_norm.shape[0] == 1
    assert seq % min(FFN_STEPS, seq) == 0 and seq % min(MIX_STEPS, seq) == 0
    assert MIX_STEPS % GLA_CHUNK == 0
    l = 0
    row = lambda a: a.reshape(1, -1).astype(F32)

    col = lambda a: a.reshape(-1, 1).astype(F32)
    h = _ffn_call(x, col(ffn1_norm[l]), ffn1_w1[l], ffn1_w3[l], ffn1_w2[l], row(final_norm),
                  batch_major_in=True, final_norm=False, bsz=bsz, seq=seq)

    assert IN_SIZES[5] == GLA_GATE_RANK and sum(IN_SIZES[:5]) == MIX_COLS
    w_in_t = jnp.swapaxes(w_in, 1, 2)[l].astype(BF16)
    gate_start = MIX_COLS + GLA_GATE_RANK
    w_mix = w_in_t[:MIX_COLS]
    w_gate = w_in_t[gate_start:gate_start + GATE_COLS]
    w_alow = jnp.pad(w_in_t[MIX_COLS:gate_start], ((0, A_LOW_PAD - GLA_GATE_RANK), (0, 0)))
    a_up_w = jnp.concatenate(
        [gla_a_up_w[l], jnp.zeros((A_LOW_PAD - GLA_GATE_RANK, GLA_KEY_WIDTH), F32)],
        axis=0).astype(BF16)
    ar, ai, bbar_re, bbar_im = _s5_discretize(
        s5_lambda_re[l], s5_lambda_im[l], s5_log_dt[l], s5_b_re[l], s5_b_im[l])
    a_re, a_im, wb, wc = _pack_s5(ar, ai, bbar_re, bbar_im, s5_c_re[l], s5_c_im[l])

    h = _mixer_call(h, row(mix_norm[l]), w_mix, w_gate, w_alow, a_re, a_im, wb, wc, row(s5_d[l]),
                    s5_glu_w[l].astype(BF16), row(s5_glu_b[l]), a_up_w, row(gla_a_up_b[l]),
                    row(gla_out_norm[l]), proj_s5[l].astype(BF16), proj_gla[l].astype(BF16),
                    w_out[l].astype(BF16), bsz=bsz, seq=seq)

    return _ffn_call(h, col(ffn2_norm[l]), ffn2_w1[l], ffn2_w3[l], ffn2_w2[l], row(final_norm),
                     batch_major_in=False, final_norm=True, bsz=bsz, seq=seq)
```

```python
import functools

import jax
import jax.numpy as jnp
from jax import lax
from jax.experimental import pallas as pl
from jax.experimental.pallas import tpu as pltpu

D_MODEL = 1024
S5_GROUP = 16
S5_WIDTH = 512
S5_GROUPS = 32
S5_STATE = 64
GLA_HEADS = 4
GLA_DV = 128
GLA_DK = 64
GLA_KEY_WIDTH = 256
GLA_VAL_WIDTH = 512
GLA_GATE_RANK = 16
GLA_TAU = 16.0
GLA_CHUNK = 64
D_FF = 2816
EPS = 1e-6
IN_SIZES = (S5_WIDTH, GLA_KEY_WIDTH, GLA_KEY_WIDTH, GLA_VAL_WIDTH, GLA_VAL_WIDTH,
            GLA_GATE_RANK, D_MODEL, D_MODEL)

LANES = 128
SUBLANES = 8
S5_BLOCKS = 4
S5_BLOCK_GROUPS = S5_GROUPS // S5_BLOCKS
S5_BLOCK_STATE = S5_BLOCK_GROUPS * S5_STATE
A_LOW_PAD = LANES
SLABS = D_MODEL // LANES
FFN_STEPS = 128
MIX_STEPS = 128
FF_CHUNKS = (768, 768, 768, 512)
assert sum(FF_CHUNKS) == D_FF
FF_PREP_COLS = 256
FF_PREP_STEPS = D_FF // FF_PREP_COLS
assert FF_PREP_STEPS * FF_PREP_COLS == D_FF
VMEM_LIMIT = 56 * 1024 * 1024

BF16 = jnp.bfloat16
F32 = jnp.float32

C_S5 = 0
C_Q = C_S5 + S5_WIDTH
C_K = C_Q + GLA_KEY_WIDTH
C_V = C_K + GLA_KEY_WIDTH
C_R = C_V + GLA_VAL_WIDTH
MIX_COLS = C_R + GLA_VAL_WIDTH
C_GS5 = 0
C_GGLA = C_GS5 + D_MODEL
GATE_COLS = C_GGLA + D_MODEL


def _dot(a, b):
    return jnp.dot(a, b, preferred_element_type=F32)


def _dot_nt(a, b):
    return lax.dot_general(a, b, (((1,), (1,)), ((), ())), preferred_element_type=F32)


def _dot_tn(a, b):
    return lax.dot_general(a, b, (((0,), (0,)), ((), ())), preferred_element_type=F32)


def _inv_rms(x):
    return lax.rsqrt(jnp.mean(x * x, axis=-1, keepdims=True) + EPS)


def _rms_norm(x, g):
    return x * _inv_rms(x) * g


def _const_spec(shape):
    nd = len(shape)
    return pl.BlockSpec(shape, lambda *_: (0,) * nd, pipeline_mode=pl.Buffered(1))


def _ffn_body(x_ref, g_ref, w1_ref, w3_ref, w2_ref, gf_ref, o_ref, w1b, w3b, w2b, *,
              batch_major_in, final_norm, bsz, steps):
    step = pl.program_id(0)
    for c in range(FF_PREP_STEPS):
        @pl.when(step == c)
        def _(c=c):
            cols = slice(c * FF_PREP_COLS, (c + 1) * FF_PREP_COLS)
            gain = g_ref[...]
            w1b[:, cols] = (gain * w1_ref[...]).astype(BF16)
            w3b[:, cols] = (gain * w3_ref[...]).astype(BF16)
            w2b[cols, :] = (0.5 * w2_ref[...]).astype(BF16)

    @pl.when(step >= FF_PREP_STEPS)
    def _():
        _ffn_tile(x_ref, w1b, w3b, w2b, gf_ref, o_ref, batch_major_in=batch_major_in,
                  final_norm=final_norm, bsz=bsz, steps=steps)


def _ffn_tile(x_ref, w1_ref, w3_ref, w2_ref, gf_ref, o_ref, *,
              batch_major_in, final_norm, bsz, steps):
    rows = bsz * steps
    if batch_major_in:
        x = x_ref[...].reshape(rows, D_MODEL)
    else:
        x = jnp.concatenate(
            [jnp.concatenate([x_ref[j, pl.ds(bi, steps, stride=bsz), :] for j in range(SLABS)],
                             axis=-1) for bi in range(bsz)], axis=0)
    xb = x.astype(BF16)
    inv = _inv_rms(x)
    acc = None
    col = 0
    for width in FF_CHUNKS:
        a = _dot(xb, w1_ref[:, col:col + width]) * inv
        b = _dot(xb, w3_ref[:, col:col + width]) * inv
        hmid = (a * jax.nn.sigmoid(a) * b).astype(BF16)
        part = _dot(hmid, w2_ref[col:col + width, :])
        acc = part if acc is None else acc + part
        col += width
    y = x + acc
    if final_norm:
        y = _rms_norm(y, gf_ref[...])
    if batch_major_in:
        for bi in range(bsz):
            for j in range(SLABS):
                o_ref[j, pl.ds(bi, steps, stride=bsz), :] = (
                    y[bi * steps:(bi + 1) * steps, j * LANES:(j + 1) * LANES])
    else:
        o_ref[...] = y.reshape(bsz, steps, D_MODEL)


def _ffn_call(x, g, w1, w3, w2, gf, *, batch_major_in, final_norm, bsz, seq):
    steps = min(FFN_STEPS, seq)
    bm_block, tm_block = (bsz, steps, D_MODEL), (SLABS, steps * bsz, LANES)
    in_block, out_block = (bm_block, tm_block) if batch_major_in else (tm_block, bm_block)
    out_shape = (SLABS, seq * bsz, LANES) if batch_major_in else (bsz, seq, D_MODEL)
    tile = lambda i: (0, jnp.maximum(i - FF_PREP_STEPS, 0), 0)
    prep = lambda i: jnp.minimum(i, FF_PREP_STEPS - 1)
    return pl.pallas_call(
        functools.partial(_ffn_body, batch_major_in=batch_major_in, final_norm=final_norm,
                          bsz=bsz, steps=steps),
        grid=(FF_PREP_STEPS + seq // steps,),
        in_specs=[pl.BlockSpec(in_block, tile),
                  _const_spec((D_MODEL, 1)),
                  pl.BlockSpec((D_MODEL, FF_PREP_COLS), lambda i: (0, prep(i))),
                  pl.BlockSpec((D_MODEL, FF_PREP_COLS), lambda i: (0, prep(i))),
                  pl.BlockSpec((FF_PREP_COLS, D_MODEL), lambda i: (prep(i), 0)),
                  _const_spec((1, D_MODEL))],
        out_specs=pl.BlockSpec(out_block, tile),
        out_shape=jax.ShapeDtypeStruct(out_shape, F32),
        scratch_shapes=[pltpu.VMEM((D_MODEL, D_FF), BF16), pltpu.VMEM((D_MODEL, D_FF), BF16),
                        pltpu.VMEM((D_FF, D_MODEL), BF16)],
        compiler_params=pltpu.CompilerParams(
            dimension_semantics=("arbitrary",), vmem_limit_bytes=VMEM_LIMIT),
        name="ffn_final" if final_norm else "ffn",
    )(x, g, w1, w3, w2, gf)


def _mixer_body(h_ref, g_ref, w_mix_ref, w_gate_ref, w_alow_ref, a_re_ref, a_im_ref, wb_ref, wc_ref,
                d_ref, glu_w_ref, glu_b_ref, a_up_w_ref, a_up_b_ref, gnorm_ref,
                p_s5_ref, p_gla_ref, w_out_ref, o_ref,
                s5_state, gla_state, bu_buf, loga_buf, bcum_buf, *, bsz, steps, chunk):
    rows = bsz * steps
    crows = bsz * chunk

    @pl.when(pl.program_id(0) == 0)
    def _():
        s5_state[...] = jnp.zeros_like(s5_state)
        gla_state[...] = jnp.zeros_like(gla_state)

    h = jnp.concatenate([h_ref[j] for j in range(SLABS)], axis=-1)
    u = (h * g_ref[...]).astype(BF16)
    inv = _inv_rms(h)

    def proj(col, width, w_ref=w_mix_ref):
        return _dot_nt(u, w_ref[col:col + width, :]) * inv

    s5_in = proj(C_S5, S5_WIDTH)
    s5_in_bf = s5_in.astype(BF16)
    y_parts = []
    for m in range(S5_BLOCKS):
        buf = bu_buf.at[m]
        buf[...] = _dot(s5_in_bf[:, m * LANES:(m + 1) * LANES], wb_ref[m])
        a_re = jnp.broadcast_to(a_re_ref[m], (bsz, S5_BLOCK_STATE))
        a_im = jnp.broadcast_to(a_im_ref[m], (bsz, S5_BLOCK_STATE))
        s_re = s5_state[m, :, :S5_BLOCK_STATE]
        s_im = s5_state[m, :, S5_BLOCK_STATE:]
        for t in range(steps):
            r0 = t * bsz
            n_re = a_re * s_re - a_im * s_im + buf[r0:r0 + bsz, :S5_BLOCK_STATE]
            n_im = a_re * s_im + a_im * s_re + buf[r0:r0 + bsz, S5_BLOCK_STATE:]
            buf[r0:r0 + bsz, :S5_BLOCK_STATE] = n_re
            buf[r0:r0 + bsz, S5_BLOCK_STATE:] = n_im
            s_re, s_im = n_re, n_im
        s5_state[m, :, :S5_BLOCK_STATE] = s_re
        s5_state[m, :, S5_BLOCK_STATE:] = s_im
        y_parts.append(_dot(buf[...].astype(BF16), wc_ref[m]))
    y = jnp.concatenate(y_parts, axis=-1) + d_ref[...] * s5_in
    z = jax.nn.gelu(y)
    y_s5 = z * jax.nn.sigmoid(_dot(z.astype(BF16), glu_w_ref[...]) + glu_b_ref[...])

    a_low = proj(0, A_LOW_PAD, w_alow_ref).astype(BF16)
    loga_buf[...] = jax.nn.log_sigmoid(_dot(a_low, a_up_w_ref[...]) + a_up_b_ref[...]) / GLA_TAU

    q = proj(C_Q, GLA_KEY_WIDTH) * (GLA_DK ** -0.5)
    k = proj(C_K, GLA_KEY_WIDTH)
    v = proj(C_V, GLA_VAL_WIDTH).astype(BF16)

    row = lax.broadcasted_iota(jnp.int32, (crows, crows), 0)
    col = lax.broadcasted_iota(jnp.int32, (crows, crows), 1)
    causal = ((row % bsz) == (col % bsz)) & (col <= row)
    own_batch = (col // GLA_DK) == (row % bsz)
    own_batch8 = own_batch[:bsz]
    key_lane = lax.broadcasted_iota(jnp.int32, (crows, GLA_KEY_WIDTH), 1) // GLA_DK

    def expand(xh):
        rep = xh + pltpu.roll(xh, 2 * GLA_DK, 1)
        rep = rep + pltpu.roll(rep, GLA_DK, 1)
        return jnp.concatenate([rep, rep], axis=-1)

    o_chunks = []
    for c in range(steps // chunk):
        c0 = c * crows
        b_last = jnp.zeros((bsz, GLA_KEY_WIDTH), F32)
        for t in range(chunk):
            r0 = c0 + t * bsz
            b_last = b_last + loga_buf[r0:r0 + bsz, :]
            bcum_buf[r0:r0 + bsz, :] = b_last
        bcum = bcum_buf[c0:c0 + crows, :]
        b_last_rows = jnp.broadcast_to(b_last[None], (chunk, bsz, GLA_KEY_WIDTH)).reshape(
            crows, GLA_KEY_WIDTH)
        k_c = k[c0:c0 + crows]
        q_t = q[c0:c0 + crows] * jnp.exp(bcum)
        k_t = (k_c * jnp.exp(-bcum)).astype(BF16)
        k_end = k_c * jnp.exp(b_last_rows - bcum)
        o_parts = []
        for hd in range(GLA_HEADS):
            head = key_lane == hd
            q_h = jnp.where(head, q_t, 0.0)
            scores = _dot_nt(q_h.astype(BF16), k_t)
            p = jnp.where(causal, scores, 0.0).astype(BF16)
            v_h = v[c0:c0 + crows, hd * GLA_DV:(hd + 1) * GLA_DV]
            q_exp = jnp.where(own_batch, expand(q_h), 0.0).astype(BF16)
            s_prev = gla_state[hd]
            o_h = _dot(jnp.concatenate([p, q_exp], axis=1),
                       jnp.concatenate([v_h, s_prev.T.astype(BF16)], axis=0))
            k_exp = jnp.where(own_batch, expand(jnp.where(head, k_end, 0.0)), 0.0).astype(BF16)
            d_state = _dot_tn(v_h, k_exp)
            bl = expand(jnp.where(head[:bsz], b_last, 0.0))
            decay = jnp.exp(jnp.sum(jnp.where(own_batch8, bl, 0.0), axis=0, keepdims=True))
            gla_state[hd] = decay * s_prev + d_state
            o_parts.append(o_h * lax.rsqrt(jnp.mean(o_h * o_h, axis=-1, keepdims=True) + EPS))
        o_chunks.append(jnp.concatenate(o_parts, axis=-1))
    o = jnp.concatenate(o_chunks, axis=0) * gnorm_ref[...]
    r = proj(C_R, GLA_VAL_WIDTH)
    y_gla = o * (r * jax.nn.sigmoid(r))

    gates = jax.nn.sigmoid(proj(0, GATE_COLS, w_gate_ref))
    g_s5 = gates[:, C_GS5:C_GS5 + D_MODEL]
    g_gla = gates[:, C_GGLA:C_GGLA + D_MODEL]
    merged = (g_s5 * _dot(y_s5.astype(BF16), p_s5_ref[...])
              + g_gla * _dot(y_gla.astype(BF16), p_gla_ref[...]))
    out = h + _dot(merged.astype(BF16), w_out_ref[...])
    for j in range(SLABS):
        o_ref[j] = out[:, j * LANES:(j + 1) * LANES]


def _mixer_call(h, g, w_mix, w_gate, w_alow, a_re, a_im, wb, wc, d_skip, glu_w, glu_b, a_up_w,
                a_up_b, gnorm, p_s5, p_gla, w_out, *, bsz, seq):
    steps = min(MIX_STEPS, seq)
    rows = bsz * steps
    args = (g, w_mix, w_gate, w_alow, a_re, a_im, wb, wc, d_skip, glu_w, glu_b, a_up_w, a_up_b,
            gnorm, p_s5, p_gla, w_out)
    tile_spec = pl.BlockSpec((SLABS, rows, LANES), lambda i: (0, i, 0))
    return pl.pallas_call(
        functools.partial(_mixer_body, bsz=bsz, steps=steps, chunk=GLA_CHUNK),
        grid=(seq // steps,),
        in_specs=[tile_spec] + [_const_spec(a.shape) for a in args],
        out_specs=tile_spec,
        out_shape=jax.ShapeDtypeStruct((SLABS, seq * bsz, LANES), F32),
        scratch_shapes=[
            pltpu.VMEM((S5_BLOCKS, bsz, 2 * S5_BLOCK_STATE), F32),
            pltpu.VMEM((GLA_HEADS, GLA_DV, bsz * GLA_DK), F32),
            pltpu.VMEM((S5_BLOCKS, rows, 2 * S5_BLOCK_STATE), F32),
            pltpu.VMEM((rows, GLA_KEY_WIDTH), F32),
            pltpu.VMEM((rows, GLA_KEY_WIDTH), F32),
        ],
        compiler_params=pltpu.CompilerParams(
            dimension_semantics=("arbitrary",), vmem_limit_bytes=VMEM_LIMIT),
        name="mixer",
    )(h, *args)


def _s5_discretize_body(lam_re_ref, lam_im_ref, log_dt_ref, b_re_ref, b_im_ref,
                        ar_ref, ai_ref, bbar_re_ref, bbar_im_ref):
    lam_re = lam_re_ref[...]
    lam_im = lam_im_ref[...]
    dt = jnp.exp(log_dt_ref[...])
    mag = jnp.exp(lam_re * dt)
    ar = mag * jnp.cos(lam_im * dt)
    ai = mag * jnp.sin(lam_im * dt)
    den = lam_re * lam_re + lam_im * lam_im
    nr = ar - 1.0
    fr = ((nr * lam_re + ai * lam_im) / den)[:, None, :]
    fi = ((ai * lam_re - nr * lam_im) / den)[:, None, :]
    ar_ref[...] = ar
    ai_ref[...] = ai
    b_re = b_re_ref[...]
    b_im = b_im_ref[...]
    bbar_re_ref[...] = fr * b_re - fi * b_im
    bbar_im_ref[...] = fr * b_im + fi * b_re


def _s5_discretize(lam_re, lam_im, log_dt, b_re, b_im):
    gp = jax.ShapeDtypeStruct((S5_GROUPS, S5_STATE), F32)
    ghp = jax.ShapeDtypeStruct((S5_GROUPS, S5_GROUP, S5_STATE), F32)
    return pl.pallas_call(
        _s5_discretize_body, out_shape=(gp, gp, ghp, ghp), name="s5_discretize",
    )(lam_re, lam_im, log_dt.reshape(S5_GROUPS, 1),
      jnp.swapaxes(b_re, 1, 2), jnp.swapaxes(b_im, 1, 2))


def _pack_s5(ar, ai, bbar_re, bbar_im, c_re, c_im):
    eye = jnp.eye(S5_BLOCK_GROUPS, dtype=F32)
    nb, ng, ns, nh = S5_BLOCKS, S5_BLOCK_GROUPS, S5_STATE, S5_GROUP
    bb = jnp.stack([bbar_re, bbar_im]).reshape(2, nb, ng, nh, ns)
    wb = jnp.einsum('rmghp,gk->mghrkp', bb, eye).reshape(nb, ng * nh, 2 * ng * ns)
    cc = jnp.stack([c_re, -c_im]).reshape(2, nb, ng, nh, ns)
    wc = jnp.einsum('rmghp,gk->mrgpkh', cc, eye).reshape(nb, 2 * ng * ns, ng * nh)
    a_re = ar.reshape(nb, 1, ng * ns)
    a_im = ai.reshape(nb, 1, ng * ns)
    return a_re, a_im, wb.astype(BF16), wc.astype(BF16)


def kernel(x, ffn1_norm, ffn1_w1, ffn1_w3, ffn1_w2, mix_norm, w_in, s5_lambda_re, s5_lambda_im, s5_log_dt, s5_b_re, s5_b_im, s5_c_re, s5_c_im, s5_d, s5_glu_w, s5_glu_b, gla_a_up_w, gla_a_up_b, gla_out_norm, proj_s5, proj_gla, w_out, ffn2_norm, ffn2_w1, ffn2_w3, ffn2_w2, final_norm):
    bsz, seq, _ = x.shape
    assert bsz == SUBLANES and seq % GLA_CHUNK == 0 and ffn1_norm.shape[0] == 1
    assert seq % min(FFN_STEPS, seq) == 0 and seq % min(MIX_STEPS, seq) == 0
    assert MIX_STEPS % GLA_CHUNK == 0
    l = 0
    row = lambda a: a.reshape(1, -1).astype(F32)

    col = lambda a: a.reshape(-1, 1).astype(F32)
    h = _ffn_call(x, col(ffn1_norm[l]), ffn1_w1[l], ffn1_w3[l], ffn1_w2[l], row(final_norm),
                  batch_major_in=True, final_norm=False, bsz=bsz, seq=seq)

    assert IN_SIZES[5] == GLA_GATE_RANK and sum(IN_SIZES[:5]) == MIX_COLS
    w_in_t = jnp.swapaxes(w_in, 1, 2)[l].astype(BF16)
    gate_start = MIX_COLS + GLA_GATE_RANK
    w_mix = w_in_t[:MIX_COLS]
    w_gate = w_in_t[gate_start:gate_start + GATE_COLS]
    w_alow = jnp.pad(w_in_t[MIX_COLS:gate_start], ((0, A_LOW_PAD - GLA_GATE_RANK), (0, 0)))
    a_up_w = jnp.concatenate(
        [gla_a_up_w[l], jnp.zeros((A_LOW_PAD - GLA_GATE_RANK, GLA_KEY_WIDTH), F32)],
        axis=0).astype(BF16)
    ar, ai, bbar_re, bbar_im = _s5_discretize(
        s5_lambda_re[l], s5_lambda_im[l], s5_log_dt[l], s5_b_re[l], s5_b_im[l])
    a_re, a_im, wb, wc = _pack_s5(ar, ai, bbar_re, bbar_im, s5_c_re[l], s5_c_im[l])

    h = _mixer_call(h, row(mix_norm[l]), w_mix, w_gate, w_alow, a_re, a_im, wb, wc, row(s5_d[l]),
                    s5_glu_w[l].astype(BF16), row(s5_glu_b[l]), a_up_w, row(gla_a_up_b[l]),
                    row(gla_out_norm[l]), proj_s5[l].astype(BF16), proj_gla[l].astype(BF16),
                    w_out[l].astype(BF16), bsz=bsz, seq=seq)

    return _ffn_call(h, col(ffn2_norm[l]), ffn2_w1[l], ffn2_w3[l], ffn2_w2[l], row(final_norm),
                     batch_major_in=False, final_norm=True, bsz=bsz, seq=seq)
```
